```python
import math
import jax
import jax.numpy as jnp
from jax import lax
import numpy as np

D_MODEL = 2048
BATCH = 16
SEQ = 256
DEPTH = 4
DEC_BATCH = 2
DEC_SEQ = 4096
PAST_LEN = 256

GRID_W = 64
HEAD_DIM = 64
MIX_WIDTH = D_MODEL // 2
N_BRANCHES = 3
SSD_HEAD_DIM = HEAD_DIM
SSD_HEADS = MIX_WIDTH // SSD_HEAD_DIM
SSD_INNER = SSD_HEADS * SSD_HEAD_DIM
SSD_GROUPS = 2
SSD_STATE = 128
SSD_CHUNK = 128
SSD_CONV = 3
SSD_XBC = SSD_INNER + 2 * SSD_GROUPS * SSD_STATE
RWKV_HEAD_DIM = HEAD_DIM
RWKV_HEADS = MIX_WIDTH // RWKV_HEAD_DIM
RWKV_INNER = RWKV_HEADS * RWKV_HEAD_DIM
RWKV_DECAY_RANK = 64
RWKV_ICLR_RANK = 64
RWKV_GATE_RANK = 160
RWKV_COLS = 3 * RWKV_INNER + 2 * RWKV_DECAY_RANK + 2 * RWKV_ICLR_RANK + RWKV_GATE_RANK
RWKV_LN_EPS = 64e-5
NA_HEAD_DIM = HEAD_DIM
NA_HEADS = MIX_WIDTH // NA_HEAD_DIM
NA_INNER = NA_HEADS * NA_HEAD_DIM
WIN_R = 8
WIN_C = 16
ATTN_BLOCK = 128
IN_SPLITS = (SSD_INNER, SSD_XBC, 2 * SSD_HEADS, RWKV_COLS, NA_INNER, NA_INNER, NA_INNER, N_BRANCHES * D_MODEL)
IN_COLS = SSD_INNER + SSD_XBC + 2 * SSD_HEADS + RWKV_COLS + 3 * NA_INNER + N_BRANCHES * D_MODEL
N_EXPERTS = 32
TOP_K = 4
D_FF = D_MODEL
SWIGLU_LIMIT = 7.0
SWIGLU_ALPHA = 1.702
MOE_BLOCK = 128
NORM_EPS = 1e-6

kernel_name = 'hybrid_flow_trunk_step'


def rmsnorm(x, g):
    x32 = x.astype(jnp.float32)
    y = x32 * lax.rsqrt(jnp.mean(x32 * x32, axis=-1, keepdims=True) + NORM_EPS)
    return (y * g.astype(jnp.float32)).astype(x.dtype)


def split_last(x, sizes):
    cuts = [int(s) for s in np.cumsum(sizes)[:-1]]
    return jnp.split(x, cuts, axis=-1)


def centred_dwconv(x, w, b):
    width, ch = w.shape
    y = lax.conv_general_dilated(x, w[:, None, :].astype(x.dtype), window_strides=(1,),
                                 padding=[(width // 2, width // 2)],
                                 dimension_numbers=('NWC', 'WIO', 'NWC'), feature_group_count=ch)
    return y + b.astype(x.dtype)


def token_shift(x, mu_prev, mu_next):
    x_prev = jnp.pad(x, ((0, 0), (1, 0), (0, 0)))[:, :-1]
    x_next = jnp.pad(x, ((0, 0), (0, 1), (0, 0)))[:, 1:]
    return x + mu_prev * (x_prev - x) + mu_next * (x_next - x)


def _flip(t, rev):
    return t[:, ::-1] if rev else t


def segsum(a):
    t = a.shape[-1]
    cs = jnp.cumsum(a, axis=-1)
    diff = cs[..., :, None] - cs[..., None, :]
    return jnp.where(jnp.tril(jnp.ones((t, t), dtype=bool)), diff, -jnp.inf)


def ssd_chunked(x, a, bm, cm, s0):
    b, t, h, p = x.shape
    n = bm.shape[-1]
    nc = t // SSD_CHUNK
    x = x.reshape(b, nc, SSD_CHUNK, h, p)
    bm = bm.reshape(b, nc, SSD_CHUNK, h, n)
    cm = cm.reshape(b, nc, SSD_CHUNK, h, n)
    a = a.reshape(b, nc, SSD_CHUNK, h).transpose(0, 3, 1, 2)
    a_cs = jnp.cumsum(a, axis=-1)
    cb = jnp.einsum('bclhn,bcshn->bhcls', cm, bm) * jnp.exp(segsum(a))
    y_diag = jnp.einsum('bhcls,bcshp->bclhp', cb, x)
    decay_to_end = jnp.exp(a_cs[..., -1:] - a_cs).transpose(0, 2, 3, 1)
    states = jnp.einsum('bclhn,bclhp->bchpn', bm * decay_to_end[..., None], x)
    states = jnp.concatenate([s0[:, None], states], axis=1)
    chunk_decay = jnp.exp(segsum(jnp.pad(a_cs[..., -1], ((0, 0), (0, 0), (1, 0)))))
    states = jnp.einsum('bhzc,bchpn->bzhpn', chunk_decay, states)
    prev_states, final = states[:, :-1], states[:, -1]
    decay_in = jnp.exp(a_cs).transpose(0, 2, 3, 1)
    y_off = jnp.einsum('bclhn,bchpn->bclhp', cm, prev_states) * decay_in[..., None]
    return (y_diag + y_off).reshape(b, t, h, p), final


def ssd_branch(z, xbc, dt_raw, lp, s0):
    b, t, _ = z.shape
    xbc = jax.nn.silu(centred_dwconv(xbc, lp['ssd_conv_w'], lp['ssd_conv_b'])).astype(jnp.float32)
    xs, bm, cm = split_last(xbc, (SSD_INNER, SSD_GROUPS * SSD_STATE, SSD_GROUPS * SSD_STATE))
    rep = SSD_HEADS // SSD_GROUPS
    xh = xs.reshape(b, t, SSD_HEADS, SSD_HEAD_DIM)
    bm = jnp.repeat(bm.reshape(b, t, SSD_GROUPS, SSD_STATE), rep, axis=2)
    cm = jnp.repeat(cm.reshape(b, t, SSD_GROUPS, SSD_STATE), rep, axis=2)
    dt = jax.nn.softplus(dt_raw.astype(jnp.float32).reshape(b, t, 2, SSD_HEADS) + lp['ssd_dt_bias'].astype(jnp.float32))
    a_neg = -jnp.exp(lp['ssd_a_log'].astype(jnp.float32))
    y = xh * lp['ssd_d'].astype(jnp.float32)[:, None]
    finals = []
    for d in range(2):
        rev = d == 1
        yd, sd = ssd_chunked(_flip(xh * dt[:, :, d, :, None], rev), _flip(dt[:, :, d] * a_neg[d], rev),
                             _flip(bm, rev), _flip(cm, rev), s0[:, d].astype(jnp.float32))
        y = y + _flip(yd, rev)
        finals.append(sd)
    y = y.reshape(b, t, SSD_INNER) * jax.nn.silu(z.astype(jnp.float32))
    yg = y.reshape(b, t, SSD_GROUPS, SSD_INNER // SSD_GROUPS)
    yg = yg * lax.rsqrt(jnp.mean(yg * yg, axis=-1, keepdims=True) + NORM_EPS)
    y = yg.reshape(b, t, SSD_INNER) * lp['ssd_norm_g'].astype(jnp.float32)
    return y, jnp.stack(finals, axis=1)


def rwkv_scan(r, w, k, v, kk, a, s0, reverse):
    def step(s, inp):
        r_t, w_t, k_t, v_t, kk_t, a_t = inp
        removed = jnp.einsum('bhvk,bhk->bhv', s, kk_t)
        s = s * w_t[:, :, None, :] - removed[..., None] * (kk_t * a_t)[:, :, None, :] + v_t[..., None] * k_t[:, :, None, :]
        return s, jnp.einsum('bhvk,bhk->bhv', s, r_t)
    xs = tuple(jnp.moveaxis(u, 1, 0) for u in (r, w, k, v, kk, a))
    s_fin, o = lax.scan(step, s0, xs, reverse=reverse)
    return jnp.moveaxis(o, 0, 1), s_fin


def rwkv_branch(p, lp, s0):
    b, t, _ = p.shape
    p = token_shift(p.astype(jnp.float32), lp['rwkv_mu'][0], lp['rwkv_mu'][1])
    r, k, v, w_lo, a_lo, g_lo = split_last(p, (RWKV_INNER, RWKV_INNER, RWKV_INNER, 2 * RWKV_DECAY_RANK,
                                              2 * RWKV_ICLR_RANK, RWKV_GATE_RANK))
    logw = lp['rwkv_w0'] + jnp.einsum('btdr,drc->btdc', jnp.tanh(w_lo.reshape(b, t, 2, RWKV_DECAY_RANK)), lp['rwkv_w2'])
    decay = jnp.exp(-jnp.exp(-jax.nn.softplus(-logw) - 0.5))
    a = jax.nn.sigmoid(lp['rwkv_a0'] + jnp.einsum('btdr,drc->btdc', a_lo.reshape(b, t, 2, RWKV_ICLR_RANK), lp['rwkv_a2']))
    g = jax.nn.sigmoid(g_lo) @ lp['rwkv_g2']
    heads = lambda u: u.reshape(u.shape[:-1] + (RWKV_HEADS, RWKV_HEAD_DIM))
    kk = heads(k * lp['rwkv_kk'])
    kk = kk / jnp.maximum(jnp.sqrt(jnp.sum(kk * kk, axis=-1, keepdims=True)), 1e-12)
    k_dir = k[:, :, None] * (1.0 + (a - 1.0) * lp['rwkv_ka'])
    rh, vh = heads(r), heads(v)
    kd, ad, wd = heads(k_dir), heads(a), heads(decay)
    o = jnp.zeros_like(rh)
    finals = []
    for d in range(2):
        od, sd = rwkv_scan(rh, wd[:, :, d], kd[:, :, d], vh, kk, ad[:, :, d], s0[:, d].astype(jnp.float32), d == 1)
        o = o + od
        finals.append(sd)
    mu = jnp.mean(o, axis=-1, keepdims=True)
    var = jnp.mean(jnp.square(o - mu), axis=-1, keepdims=True)
    o = ((o - mu) * lax.rsqrt(var + RWKV_LN_EPS)).reshape(b, t, RWKV_INNER) * lp['rwkv_ln_g'] + lp['rwkv_ln_b']
    bonus = jnp.sum(rh * jnp.sum(kd, axis=2) * lp['rwkv_rk'], axis=-1, keepdims=True) * vh
    y = (o + bonus.reshape(b, t, RWKV_INNER)) * g
    return y, jnp.stack(finals, axis=1)


def context_attention(q, k, v):
    b, t, h, d = q.shape
    scale = NA_HEAD_DIM ** -0.5
    qb = q.reshape(b, t // ATTN_BLOCK, ATTN_BLOCK, h, d).transpose(1, 0, 2, 3, 4)
    def blk(q_r):
        s = jnp.einsum('bqhd,bkhd->bhqk', q_r, k).astype(jnp.float32) * scale
        return jnp.einsum('bhqk,bkhd->bqhd', jax.nn.softmax(s, axis=-1).astype(v.dtype), v)
    o = lax.map(blk, qb)
    return o.transpose(1, 0, 2, 3, 4).reshape(b, t, h * d)


def neighbourhood_attention(q, k, v, k_ctx, v_ctx, rpb):
    b, t, h, d = q.shape
    rows = t // GRID_W
    wr = min(WIN_R, rows)
    n_win = wr * WIN_C
    r_ids = np.arange(rows)
    c_ids = np.arange(GRID_W)
    key_r = np.clip(r_ids - wr // 2, 0, rows - wr)[:, None] + np.arange(wr)
    key_c = np.clip(c_ids - WIN_C // 2, 0, GRID_W - WIN_C)[:, None] + np.arange(WIN_C)
    key_idx = (key_r[:, None, :, None] * GRID_W + key_c[None, :, None, :]).reshape(rows, GRID_W, n_win).astype(np.int32)
    off_r = (key_r - r_ids[:, None] + WIN_R - 1)[:, None, :, None]
    off_c = (key_c - c_ids[:, None] + WIN_C - 1)[None, :, None, :]
    bias = rpb[:, off_r, off_c].astype(jnp.float32)
    bias = bias.reshape(h, rows, GRID_W, n_win).transpose(1, 0, 2, 3)
    qb = q.reshape(b, rows, GRID_W, h, d).transpose(1, 0, 2, 3, 4)
    scale = NA_HEAD_DIM ** -0.5
    def row_block(args):
        q_r, idx_r, bias_r = args
        kb = k[:, idx_r]
        vb = v[:, idx_r]
        s_win = jnp.einsum('bqhd,bqnhd->bhqn', q_r, kb).astype(jnp.float32) * scale + bias_r[None]
        s_ctx = jnp.einsum('bqhd,bchd->bhqc', q_r, k_ctx).astype(jnp.float32) * scale
        pr = jax.nn.softmax(jnp.concatenate([s_win, s_ctx], axis=-1), axis=-1).astype(v.dtype)
        return (jnp.einsum('bhqn,bqnhd->bqhd', pr[..., :n_win], vb)
                + jnp.einsum('bhqc,bchd->bqhd', pr[..., n_win:], v_ctx))
    o = lax.map(row_block, (qb, jnp.asarray(key_idx), bias))
    return o.transpose(1, 0, 2, 3, 4).reshape(b, t, h * d)


def token_mixing(h, lp, cache):
    b, t, _ = h.shape
    p = h @ lp['w_in']
    z, xbc, dt_raw, p_rwkv, q, k, v, p_gate = split_last(p, IN_SPLITS)
    if cache is None:
        s_ssd0 = jnp.zeros((b, 2, SSD_HEADS, SSD_HEAD_DIM, SSD_STATE), jnp.float32)
        s_rwkv0 = jnp.zeros((b, 2, RWKV_HEADS, RWKV_HEAD_DIM, RWKV_HEAD_DIM), jnp.float32)
    else:
        k_ctx, v_ctx, s_ssd0, s_rwkv0 = cache
    y_ssd, s_ssd = ssd_branch(z, xbc, dt_raw, lp, s_ssd0)
    y_rwkv, s_rwkv = rwkv_branch(p_rwkv, lp, s_rwkv0)
    qh = q.reshape(b, t, NA_HEADS, NA_HEAD_DIM)
    kh = k.reshape(b, t, NA_HEADS, NA_HEAD_DIM)
    vh = v.reshape(b, t, NA_HEADS, NA_HEAD_DIM)
    if cache is None:
        y_na = context_attention(qh, kh, vh)
    else:
        y_na = neighbourhood_attention(qh, kh, vh, k_ctx.astype(h.dtype), v_ctx.astype(h.dtype), lp['na_rpb'])
    branches = jnp.stack([y_ssd.astype(h.dtype), y_rwkv.astype(h.dtype), y_na.astype(h.dtype)], axis=2)
    proj = jnp.einsum('btiw,iwd->btid', branches, lp['w_branch'])
    gates = jax.nn.sigmoid(p_gate.reshape(b, t, N_BRANCHES, D_MODEL))
    out = jnp.sum(gates * proj, axis=2) @ lp['w_out']
    return out, (kh, vh, s_ssd, s_rwkv)


def moe_ffn(h, router_w, router_b, w_gu, b_gu, w_down, b_down):
    b, t, dm = h.shape
    n_tok = b * t
    xf = h.reshape(n_tok, dm)
    logits = (xf @ router_w + router_b).astype(jnp.float32)
    top_v, top_i = lax.top_k(logits, TOP_K)
    probs = jax.nn.softmax(top_v, axis=-1)
    flat_e = top_i.reshape(-1)
    order = jnp.argsort(flat_e)
    sorted_e = flat_e[order]
    tok = (order // TOP_K).astype(jnp.int32)
    wts = probs.reshape(-1)[order]
    counts = jnp.bincount(flat_e, length=N_EXPERTS)
    padded = (counts + MOE_BLOCK - 1) // MOE_BLOCK * MOE_BLOCK
    pad_end = jnp.cumsum(padded)
    rank = jnp.arange(n_tok * TOP_K) - (jnp.cumsum(counts) - counts)[sorted_e]
    slot = (pad_end - padded)[sorted_e] + rank
    n_blocks = -(-(n_tok * TOP_K) // MOE_BLOCK) + N_EXPERTS
    cap = n_blocks * MOE_BLOCK
    slot_tok = jnp.full((cap,), n_tok, jnp.int32).at[slot].set(tok)
    slot_w = jnp.zeros((cap,), jnp.float32).at[slot].set(wts)
    block_e = jnp.minimum(jnp.searchsorted(pad_end, jnp.arange(n_blocks) * MOE_BLOCK, side='right'), N_EXPERTS - 1)
    x_pad = jnp.concatenate([xf, jnp.zeros((1, dm), xf.dtype)], axis=0)
    xb = x_pad[slot_tok].reshape(n_blocks, MOE_BLOCK, dm)
    def expert_block(args):
        xblk, e = args
        gate, up = jnp.split(xblk @ w_gu[e] + b_gu[e], 2, axis=-1)
        gate = jnp.minimum(gate, SWIGLU_LIMIT)
        up = jnp.clip(up, -SWIGLU_LIMIT, SWIGLU_LIMIT)
        act = gate * jax.nn.sigmoid(gate * SWIGLU_ALPHA) * (up + 1.0)
        return act @ w_down[e] + b_down[e]
    yb = lax.map(expert_block, (xb, block_e))
    y = jnp.zeros((n_tok + 1, dm), h.dtype).at[slot_tok].add(yb.reshape(cap, dm) * slot_w[:, None].astype(yb.dtype))
    return y[:n_tok].reshape(b, t, dm)


def trunk_layer(x, mod, lp, cache):
    sh1, sc1, g1, sh2, sc2, g2 = jnp.split(mod[:, None, :].astype(x.dtype), 6, axis=-1)
    hn = rmsnorm(x, lp['norm1_g']) * (1.0 + sc1) + sh1
    mix, ctx = token_mixing(hn, lp, cache)
    x = x + g1 * mix
    hn = rmsnorm(x, lp['norm2_g']) * (1.0 + sc2) + sh2
    x = x + g2 * moe_ffn(hn, lp['router_w'], lp['router_b'], lp['moe_w_gu'], lp['moe_b_gu'], lp['moe_w_down'], lp['moe_b_down'])
    return x, ctx


def setup_inputs(seed: int = 0) -> dict:
    key = jax.random.key(seed)
    ks = iter(jax.random.split(key, 64))
    def nrm(shape, scale):
        return jax.random.normal(next(ks), shape, jnp.float32) * scale
    def uni(shape, lo, hi):
        return jax.random.uniform(next(ks), shape, jnp.float32, minval=lo, maxval=hi)
    L, D, E, F = DEPTH, D_MODEL, N_EXPERTS, D_FF
    dt0 = jnp.exp(uni((L, 2, SSD_HEADS), math.log(1e-3), math.log(1e-1)))
    return {
        'x_prompt': nrm((BATCH, SEQ, D), 1.0),
        'x_sample': nrm((DEC_BATCH, DEC_SEQ, D), 1.0),
        'cache_na_k': nrm((DEC_BATCH, L, PAST_LEN, NA_HEADS, NA_HEAD_DIM), 1.0),
        'cache_na_v': nrm((DEC_BATCH, L, PAST_LEN, NA_HEADS, NA_HEAD_DIM), 1.0),
        'state_ssd': nrm((DEC_BATCH, L, 2, SSD_HEADS, SSD_HEAD_DIM, SSD_STATE), 0.5),
        'state_rwkv': nrm((DEC_BATCH, L, 2, RWKV_HEADS, RWKV_HEAD_DIM, RWKV_HEAD_DIM), 0.5),
        'c': nrm((DEC_BATCH, D), 1.0),
        'c_ctx': nrm((D,), 1.0),
        'norm1_g': 1.0 + nrm((L, D), 0.02),
        'norm2_g': 1.0 + nrm((L, D), 0.02),
        'w_mod': nrm((L, D, 6 * D), 0.5 * D ** -0.5),
        'b_mod': nrm((L, 6 * D), 0.02),
        'w_in': nrm((L, D, IN_COLS), D ** -0.5),
        'ssd_conv_w': nrm((L, SSD_CONV, SSD_XBC), SSD_CONV ** -0.5),
        'ssd_conv_b': nrm((L, SSD_XBC), 0.02),
        'ssd_dt_bias': dt0 + jnp.log(-jnp.expm1(-dt0)),
        'ssd_a_log': jnp.log(uni((L, 2, SSD_HEADS), 1.0, 16.0)),
        'ssd_d': 1.0 + nrm((L, SSD_HEADS), 0.1),
        'ssd_norm_g': 1.0 + nrm((L, SSD_INNER), 0.02),
        'rwkv_mu': uni((L, 2, RWKV_COLS), 0.0, 0.5),
        'rwkv_w0': uni((L, 2, RWKV_INNER), -5.0, 1.0),
        'rwkv_w2': nrm((L, 2, RWKV_DECAY_RANK, RWKV_INNER), 0.5 * RWKV_DECAY_RANK ** -0.5),
        'rwkv_a0': nrm((L, 2, RWKV_INNER), 0.1),
        'rwkv_a2': nrm((L, 2, RWKV_ICLR_RANK, RWKV_INNER), 0.5 * RWKV_ICLR_RANK ** -0.5),
        'rwkv_g2': nrm((L, RWKV_GATE_RANK, RWKV_INNER), RWKV_GATE_RANK ** -0.5),
        'rwkv_kk': 0.85 + nrm((L, RWKV_INNER), 0.02),
        'rwkv_ka': 1.0 + nrm((L, RWKV_INNER), 0.02),
        'rwkv_rk': nrm((L, RWKV_HEADS, RWKV_HEAD_DIM), 0.1),
        'rwkv_ln_g': 1.0 + nrm((L, RWKV_INNER), 0.02),
        'rwkv_ln_b': nrm((L, RWKV_INNER), 0.02),
        'na_rpb': nrm((L, NA_HEADS, 2 * WIN_R - 1, 2 * WIN_C - 1), 0.1),
        'w_branch': nrm((L, N_BRANCHES, MIX_WIDTH, D), MIX_WIDTH ** -0.5),
        'w_out': nrm((L, D, D), D ** -0.5),
        'router_w': nrm((L, D, E), D ** -0.5),
        'router_b': nrm((L, E), 0.01),
        'moe_w_gu': nrm((L, E, D, 2 * F), D ** -0.5),
        'moe_b_gu': nrm((L, E, 2 * F), 0.01),
        'moe_w_down': nrm((L, E, F, D), F ** -0.5),
        'moe_b_down': nrm((L, E, D), 0.01),
        'final_g': 1.0 + nrm((D,), 0.02),
    }


def reference(x_prompt, x_sample, cache_na_k, cache_na_v, state_ssd, state_rwkv, c, c_ctx,
              norm1_g, norm2_g, w_mod, b_mod, w_in, ssd_conv_w, ssd_conv_b, ssd_dt_bias, ssd_a_log,
              ssd_d, ssd_norm_g, rwkv_mu, rwkv_w0, rwkv_w2, rwkv_a0, rwkv_a2, rwkv_g2, rwkv_kk, rwkv_ka,
              rwkv_rk, rwkv_ln_g, rwkv_ln_b, na_rpb, w_branch, w_out, router_w, router_b,
              moe_w_gu, moe_b_gu, moe_w_down, moe_b_down, final_g):
    ctx_k, ctx_v, ctx_ssd, ctx_rwkv = [], [], [], []
    for l in range(DEPTH):
        lp = {
            'norm1_g': norm1_g[l], 'norm2_g': norm2_g[l], 'w_in': w_in[l],
            'ssd_conv_w': ssd_conv_w[l], 'ssd_conv_b': ssd_conv_b[l], 'ssd_dt_bias': ssd_dt_bias[l],
            'ssd_a_log': ssd_a_log[l], 'ssd_d': ssd_d[l], 'ssd_norm_g': ssd_norm_g[l],
            'rwkv_mu': rwkv_mu[l], 'rwkv_w0': rwkv_w0[l], 'rwkv_w2': rwkv_w2[l], 'rwkv_a0': rwkv_a0[l],
            'rwkv_a2': rwkv_a2[l], 'rwkv_g2': rwkv_g2[l], 'rwkv_kk': rwkv_kk[l], 'rwkv_ka': rwkv_ka[l],
            'rwkv_rk': rwkv_rk[l], 'rwkv_ln_g': rwkv_ln_g[l], 'rwkv_ln_b': rwkv_ln_b[l],
            'na_rpb': na_rpb[l], 'w_branch': w_branch[l], 'w_out': w_out[l],
            'router_w': router_w[l], 'router_b': router_b[l], 'moe_w_gu': moe_w_gu[l],
            'moe_b_gu': moe_b_gu[l], 'moe_w_down': moe_w_down[l], 'moe_b_down': moe_b_down[l],
        }
        mod_ctx = jax.nn.silu(c_ctx)[None] @ w_mod[l] + b_mod[l]
        mod_lat = jax.nn.silu(c) @ w_mod[l] + b_mod[l]
        x_prompt, (kc, vc, sc, rc) = trunk_layer(x_prompt, mod_ctx, lp, None)
        ctx_k.append(kc)
        ctx_v.append(vc)
        ctx_ssd.append(sc)
        ctx_rwkv.append(rc)
        x_sample, _ = trunk_layer(x_sample, mod_lat, lp,
                                  (cache_na_k[:, l], cache_na_v[:, l], state_ssd[:, l], state_rwkv[:, l]))
    y_prompt = rmsnorm(x_prompt, final_g)
    y_sample = rmsnorm(x_sample, final_g)
    new_na_k = jnp.stack(ctx_k, axis=1)
    new_na_v = jnp.stack(ctx_v, axis=1)
    new_state_ssd = jnp.stack(ctx_ssd, axis=1)
    new_state_rwkv = jnp.stack(ctx_rwkv, axis=1)
    return (y_prompt, y_sample, new_na_k, new_na_v, new_state_ssd, new_state_rwkv)
```

```python
import functools
import math

import jax
import jax.numpy as jnp
import numpy as np
from jax import lax
from jax.experimental import pallas as pl
from jax.experimental.pallas import tpu as pltpu

F32 = jnp.float32
BF16 = jnp.bfloat16

D_MODEL = 2048
DEPTH = 4
GRID_W = 64
HEAD_DIM = 64
MIX_WIDTH = D_MODEL // 2
N_BRANCHES = 3
SSD_HEADS = MIX_WIDTH // HEAD_DIM
SSD_INNER = MIX_WIDTH
SSD_GROUPS = 2
SSD_STATE = 128
SSD_CHUNK = 128
SSD_CONV = 3
SSD_XBC = SSD_INNER + 2 * SSD_GROUPS * SSD_STATE
RWKV_HEADS = MIX_WIDTH // HEAD_DIM
RWKV_INNER = MIX_WIDTH
RWKV_DECAY_RANK = 64
RWKV_ICLR_RANK = 64
RWKV_GATE_RANK = 160
RWKV_COLS = 3 * RWKV_INNER + 2 * RWKV_DECAY_RANK + 2 * RWKV_ICLR_RANK + RWKV_GATE_RANK
RWKV_LN_EPS = 64e-5
NA_HEADS = MIX_WIDTH // HEAD_DIM
NA_INNER = MIX_WIDTH
WIN_R = 8
WIN_C = 16
N_EXPERTS = 32
TOP_K = 4
D_FF = D_MODEL
SWIGLU_LIMIT = 7.0
SWIGLU_ALPHA = 1.702
NORM_EPS = 1e-6

LANES = 128
SUBLANES = 8
VMEM_LIMIT = 56 * 1024 * 1024

RWKV_PAD = 3584
DT_PAD = LANES
COL_Z = 0
COL_XBC = COL_Z + SSD_INNER
COL_RWKV = COL_XBC + SSD_XBC
COL_DT = COL_RWKV + RWKV_PAD
COL_Q = COL_DT + DT_PAD
COL_K = COL_Q + NA_INNER
COL_V = COL_K + NA_INNER
IN_TN = 512
COL_QKV_END = COL_V + NA_INNER
COL_GATE = -(-COL_QKV_END // IN_TN) * IN_TN
IN_COLS_USED = COL_GATE + N_BRANCHES * D_MODEL
IN_COLS_PAD = -(-IN_COLS_USED // IN_TN) * IN_TN

RWKV_CHUNK = 64
DECAY_SCALE = math.exp(-0.5)


def _cparams(sem):
    return pltpu.CompilerParams(dimension_semantics=sem, vmem_limit_bytes=VMEM_LIMIT)


def _dot(a, b):
    return jnp.dot(a, b, preferred_element_type=F32)


def _dot_nt(a, b):
    return lax.dot_general(a, b, (((1,), (1,)), ((), ())), preferred_element_type=F32)


def _split3(x):
    hi = x.astype(BF16)
    r1 = x - hi.astype(F32)
    mid = r1.astype(BF16)
    lo = (r1 - mid.astype(F32)).astype(BF16)
    return hi, mid, lo


def _dot_const_lhs3(c_bf16, x):
    hi, mid, lo = _split3(x)
    return _dot(c_bf16, hi) + _dot(c_bf16, mid) + _dot(c_bf16, lo)


def _dot_const_rhs2(x, c_bf16):
    hi = x.astype(BF16)
    lo = (x - hi.astype(F32)).astype(BF16)
    return _dot(hi, c_bf16) + _dot(lo, c_bf16)


def _sigmoid(x):
    return 1.0 / (1.0 + jnp.exp(-x))


def _silu(x):
    return x * _sigmoid(x)


def _mod_kernel(c_ref, w_ref, b_ref, o_ref):
    a = _silu(c_ref[...]).astype(BF16)
    o_ref[...] = _dot(a, w_ref[...].astype(BF16)) + b_ref[...]


def _modulation(cond, w_mod, b_mod):
    n = w_mod.shape[1]
    tn = 1024
    return pl.pallas_call(
        _mod_kernel,
        grid=(n // tn,),
        in_specs=[pl.BlockSpec((SUBLANES, D_MODEL), lambda j: (0, 0)),
                  pl.BlockSpec((D_MODEL, tn), lambda j: (0, j)),
                  pl.BlockSpec((1, tn), lambda j: (0, j))],
        out_specs=pl.BlockSpec((SUBLANES, tn), lambda j: (0, j)),
        out_shape=jax.ShapeDtypeStruct((SUBLANES, n), F32),
        compiler_params=_cparams(("arbitrary",)),
        name="adaln_modulation",
    )(cond, w_mod, b_mod.reshape(1, n))


def _norm_matmul_kernel(x_ref, g_ref, mod_ref, w_ref, o_ref, h_ref, *, shift_row, scale_row):
    @pl.when(pl.program_id(1) == 0)
    def _():
        x = x_ref[...]
        y = x * lax.rsqrt(jnp.mean(x * x, axis=-1, keepdims=True) + NORM_EPS)
        y = y * g_ref[...]
        m = mod_ref[0]
        y = y * (1.0 + m[scale_row:scale_row + 1, :]) + m[shift_row:shift_row + 1, :]
        h_ref[...] = y.astype(BF16)

    o_ref[...] = _dot(h_ref[...], w_ref[...]).astype(o_ref.dtype)


def _norm_matmul(x, g, mod, w, seg_tokens, shift_row, scale_row, tm, tn, out_dtype=F32):
    m, d = x.shape
    n = w.shape[1]
    kern = functools.partial(_norm_matmul_kernel, shift_row=shift_row, scale_row=scale_row)
    return pl.pallas_call(
        kern,
        grid=(m // tm, n // tn),
        in_specs=[pl.BlockSpec((tm, d), lambda i, j: (i, 0)),
                  pl.BlockSpec((1, d), lambda i, j: (0, 0)),
                  pl.BlockSpec((1, SUBLANES, d), lambda i, j: ((i * tm) // seg_tokens, 0, 0)),
                  pl.BlockSpec((d, tn), lambda i, j: (0, j))],
        out_specs=pl.BlockSpec((tm, tn), lambda i, j: (i, j)),
        out_shape=jax.ShapeDtypeStruct((m, n), out_dtype),
        scratch_shapes=[pltpu.VMEM((tm, d), BF16)],
        compiler_params=_cparams(("arbitrary", "arbitrary")),
        name="norm_matmul",
    )(x, g, mod, w)


def _shifted(x, prev8, next8, mu_p, mu_n, has_prev, has_next):
    tm = x.shape[0]
    rows = lax.broadcasted_iota(jnp.int32, x.shape, 0)
    p_row = jnp.where(has_prev, prev8[SUBLANES - 1:SUBLANES, :], 0.0)
    n_row = jnp.where(has_next, next8[0:1, :], 0.0)
    x_prev = jnp.where(rows == 0, p_row, pltpu.roll(x, 1, axis=0))
    x_next = jnp.where(rows == tm - 1, n_row, pltpu.roll(x, tm - 1, axis=0))
    return x + mu_p * (x_prev - x) + mu_n * (x_next - x)


def _rwkv_prep_kernel(
        r_ref, r_p, r_n, k_ref, k_p, k_n, v_ref, v_p, v_n,
        wl_ref, wl_p, wl_n, al_ref, al_p, al_n, gl_ref, gl_p, gl_n,
        mu_r, mu_k, mu_v, mu_wl, mu_al, mu_gl,
        w0_ref, w2_ref, a0_ref, a2_ref, g2_ref, kkp_ref, kap_ref, rk_ref,
        tril_ref, triu_ref, ones_ref, seg_ref,
        alpha_o, beta_o, kappa_o, rho_o, kappa_p_o, beta_p_o, v_o, bonus_o, g_o, pc_o,
        *, tm, n_ctx_tokens, ctx_len, lat_len):
    i = pl.program_id(0)
    tok0 = i * tm
    in_ctx = tok0 < n_ctx_tokens
    pos = jnp.where(in_ctx, tok0 % ctx_len, (tok0 - n_ctx_tokens) % lat_len)
    seq_len = jnp.where(in_ctx, ctx_len, lat_len)
    has_prev = pos != 0
    has_next = pos + tm != seq_len

    def sh(ref, p, n, mu):
        return _shifted(ref[...], p[...], n[...], mu[0:1, :], mu[1:2, :], has_prev, has_next)

    r = sh(r_ref, r_p, r_n, mu_r)
    k = sh(k_ref, k_p, k_n, mu_k)
    v = sh(v_ref, v_p, v_n, mu_v)
    w_lo = sh(wl_ref, wl_p, wl_n, mu_wl)
    a_lo = sh(al_ref, al_p, al_n, mu_al)
    g_lo = sh(gl_ref, gl_p, gl_n, mu_gl)

    seg = seg_ref[...]
    kk = k * kkp_ref[...]
    ss = _dot_const_rhs2(kk * kk, seg)
    kk = kk / jnp.maximum(jnp.sqrt(ss), 1e-12)

    tw = jnp.tanh(w_lo).astype(BF16)
    ab = a_lo.astype(BF16)
    g_o[...] = _dot(_sigmoid(g_lo).astype(BF16), g2_ref[...])
    v_o[...] = v.astype(BF16)

    tri = (tril_ref[...], triu_ref[...])
    ones_blk = ones_ref[...]
    kap = kap_ref[...]
    k_sum = jnp.zeros_like(k)
    nch = tm // RWKV_CHUNK
    for d in range(2):
        logw = w0_ref[d] + _dot(tw, w2_ref[d])
        lw = -DECAY_SCALE * _sigmoid(logw)
        a = _sigmoid(a0_ref[d] + _dot(ab, a2_ref[d]))
        b = kk * a
        k_d = k * (1.0 + (a - 1.0) * kap)
        k_sum = k_sum + k_d
        cum = _dot_const_lhs3(tri[d], lw)
        tot = _dot_const_lhs3(ones_blk, lw)
        e_neg = jnp.exp(-cum)
        e_rest = jnp.exp(tot - cum)
        alpha_o[d] = (kk * jnp.exp(cum - lw)).astype(BF16)
        beta_o[d] = (b * e_neg).astype(BF16)
        kappa_o[d] = (k_d * e_neg).astype(BF16)
        rho_o[d] = (r * jnp.exp(cum)).astype(BF16)
        kappa_p_o[d] = (k_d * e_rest).astype(BF16)
        beta_p_o[d] = (b * e_rest).astype(BF16)
        pcs = jnp.exp(tot)
        for c in range(nch):
            pc_o[d, c] = pcs[c * RWKV_CHUNK:c * RWKV_CHUNK + 1, :]
    bonus_o[...] = _dot_const_rhs2(r * k_sum * rk_ref[...], seg) * v


def _rwkv_prep(p, lw, *, tm, n_ctx_tokens, ctx_len, lat_len):
    m = p.shape[0]
    nlb = RWKV_INNER // LANES
    hb = tm // SUBLANES
    last8 = m // SUBLANES - 1
    cb = COL_RWKV // LANES

    def tile(col_blk, width=LANES, per_j=True):
        wb = width // LANES
        if per_j:
            cm = lambda i, j: (i, col_blk + j)
            cp = lambda i, j: (jnp.maximum(i * hb - 1, 0), col_blk + j)
            cn = lambda i, j: (jnp.minimum((i + 1) * hb, last8), col_blk + j)
        else:
            cm = lambda i, j: (i, col_blk // wb)
            cp = lambda i, j: (jnp.maximum(i * hb - 1, 0), col_blk // wb)
            cn = lambda i, j: (jnp.minimum((i + 1) * hb, last8), col_blk // wb)
        return [pl.BlockSpec((tm, width), cm), pl.BlockSpec((SUBLANES, width), cp),
                pl.BlockSpec((SUBLANES, width), cn)]

    gw = 2 * LANES
    in_specs = (tile(cb) + tile(cb + nlb) + tile(cb + 2 * nlb)
                + tile(cb + 3 * nlb, per_j=False) + tile(cb + 3 * nlb + 1, per_j=False)
                + tile(cb + 3 * nlb + 2, width=gw, per_j=False))
    per_lane = lambda rows: pl.BlockSpec((rows, LANES), lambda i, j: (0, j))
    const2 = lambda a, b: pl.BlockSpec((a, b), lambda i, j: (0, 0))
    in_specs += [per_lane(2), per_lane(2), per_lane(2), const2(2, LANES), const2(2, LANES), const2(2, gw)]
    in_specs += [pl.BlockSpec((2, 1, LANES), lambda i, j: (0, 0, j)),
                 pl.BlockSpec((2, LANES, LANES), lambda i, j: (0, 0, j)),
                 pl.BlockSpec((2, 1, LANES), lambda i, j: (0, 0, j)),
                 pl.BlockSpec((2, LANES, LANES), lambda i, j: (0, 0, j)),
                 pl.BlockSpec((gw, LANES), lambda i, j: (0, j)),
                 per_lane(1), per_lane(1), per_lane(1),
                 const2(tm, tm), const2(tm, tm), const2(tm, tm), const2(LANES, LANES)]
    tok_d = lambda: pl.BlockSpec((2, tm, LANES), lambda i, j: (0, i, j))
    tok = lambda: pl.BlockSpec((tm, LANES), lambda i, j: (i, j))
    nch = tm // RWKV_CHUNK
    out_specs = [tok_d() for _ in range(6)] + [tok(), tok(), tok(),
                 pl.BlockSpec((2, nch, 1, LANES), lambda i, j: (0, i, 0, j))]
    sd = jax.ShapeDtypeStruct
    out_shape = [sd((2, m, RWKV_INNER), BF16) for _ in range(6)] + [
        sd((m, RWKV_INNER), BF16), sd((m, RWKV_INNER), F32), sd((m, RWKV_INNER), F32),
        sd((2, m // RWKV_CHUNK, 1, RWKV_INNER), F32)]
    kern = functools.partial(_rwkv_prep_kernel, tm=tm, n_ctx_tokens=n_ctx_tokens, ctx_len=ctx_len, lat_len=lat_len)
    args = [p] * 18 + [lw['mu_r'], lw['mu_k'], lw['mu_v'], lw['mu_wl'], lw['mu_al'], lw['mu_gl'],
                       lw['w0'], lw['w2'], lw['a0'], lw['a2'], lw['g2'], lw['kk'], lw['ka'], lw['rk'],
                       lw['tril'], lw['triu'], lw['ones'], lw['seg']]
    return pl.pallas_call(
        kern, grid=(m // tm, nlb), in_specs=in_specs, out_specs=out_specs, out_shape=out_shape,
        compiler_params=_cparams(("arbitrary", "arbitrary")), name="rwkv_prep",
    )(*args)


def _rwkv_scan_kernel(tab_ref,
                      al_f, be_f, ka_f, rh_f, kp_f, bp_f, v_f, pc_f,
                      al_b, be_b, ka_b, rh_b, kp_b, bp_b, v_b, pc_b,
                      s0_ref, of_ref, ob_ref, sfin_ref, s_scr, *, pairs):
    s = pl.program_id(1)
    is_first = tab_ref[2, s] == 1
    is_last = tab_ref[3, s] == 1

    @pl.when(is_first)
    def _():
        s_scr[...] = s0_ref[0]

    c2 = 2 * RWKV_CHUNK
    lane = lax.broadcasted_iota(jnp.int32, (1, LANES), 1)
    m0 = lane < HEAD_DIM
    ri = lax.broadcasted_iota(jnp.int32, (c2, c2), 0)
    ci = lax.broadcasted_iota(jnp.int32, (c2, c2), 1)
    eye = (ri == ci).astype(F32)
    same_head = (ri // HEAD_DIM) == (ci // HEAD_DIM)
    zero_b = jnp.zeros((), BF16)

    def stack(x):
        z = jnp.zeros_like(x)
        return jnp.concatenate([jnp.where(m0, x, z), jnp.where(m0, z, x)], axis=0)

    def fold(x):
        return x[:RWKV_CHUNK] + x[RWKV_CHUNK:]

    dirs = ((al_f, be_f, ka_f, rh_f, kp_f, bp_f, v_f, pc_f, of_ref, ci < ri, ci <= ri),
            (al_b, be_b, ka_b, rh_b, kp_b, bp_b, v_b, pc_b, ob_ref, ci > ri, ci >= ri))
    for d, (al_r, be_r, ka_r, rh_r, kp_r, bp_r, v_r, pc_r, o_r, strict, incl) in enumerate(dirs):
        for g in range(pairs):
            sl = slice(g * LANES, (g + 1) * LANES)
            al, be, ka, rh = al_r[0, :, sl], be_r[0, :, sl], ka_r[0, :, sl], rh_r[0, :, sl]
            kp, bp, v = kp_r[0, :, sl], bp_r[0, :, sl], v_r[:, sl]
            pc = pc_r[0, 0, :, sl]
            s_bd = s_scr[d, g]
            lhs = jnp.concatenate([stack(al), stack(rh)], axis=0)
            sb = _dot_nt(lhs, stack(be))
            sk = _dot_nt(lhs, stack(ka))
            l_b = jnp.where(strict, sb[:c2], 0.0)
            l_k = jnp.where(strict, sk[:c2], 0.0)
            a_b = jnp.where(incl, sb[c2:], 0.0)
            a_k = jnp.where(incl, sk[c2:], 0.0)
            x = -l_b
            t = eye + x
            for _ in range(5):
                xb = x.astype(BF16)
                x = _dot(xb, xb)
                t = t + _dot(t.astype(BF16), x.astype(BF16))
            m2 = _dot_nt(jnp.concatenate([al, rh], axis=0), s_bd.astype(BF16))
            lk_ak = jnp.concatenate([fold(l_k), fold(a_k)], axis=0).astype(BF16)
            m3 = _dot(lk_ak, stack(v))
            w1 = m2[:RWKV_CHUNK] + m3[:RWKV_CHUNK]
            u = fold(_dot(t.astype(BF16), stack(w1.astype(BF16))))
            ub = u.astype(BF16)
            o = m2[RWKV_CHUNK:] + m3[RWKV_CHUNK:] - _dot(fold(a_b).astype(BF16), stack(ub))
            o_r[:, sl] = o
            vu_t = jnp.concatenate([v.astype(F32), u], axis=0).T.astype(BF16)
            upd = _dot(vu_t, jnp.concatenate([kp, -bp], axis=0))
            s_scr[d, g] = s_bd * pc + jnp.where(same_head, upd, 0.0)

    @pl.when(is_last)
    def _():
        sfin_ref[0] = s_scr[...]


def _rwkv_scan(ops, v, pc, s0, tables, *, pairs):
    m = v.shape[0]
    n_seq = s0.shape[0]
    steps = tables.shape[1]
    gl = pairs * LANES
    ng = RWKV_INNER // gl

    def dspec(d, row):
        return pl.BlockSpec((1, RWKV_CHUNK, gl), lambda g, s, tab: (d, tab[row, s], g))

    def dir_specs(d, row):
        return [dspec(d, row) for _ in range(6)] + [
            pl.BlockSpec((RWKV_CHUNK, gl), lambda g, s, tab: (tab[row, s], g)),
            pl.BlockSpec((1, 1, 1, gl), lambda g, s, tab: (d, tab[row, s], 0, g))]

    in_specs = dir_specs(0, 0) + dir_specs(1, 1) + [
        pl.BlockSpec((1, 2, pairs, LANES, LANES), lambda g, s, tab: (tab[4, s], 0, g, 0, 0))]
    out_specs = [pl.BlockSpec((RWKV_CHUNK, gl), lambda g, s, tab: (tab[0, s], g)),
                 pl.BlockSpec((RWKV_CHUNK, gl), lambda g, s, tab: (tab[1, s], g)),
                 pl.BlockSpec((1, 2, pairs, LANES, LANES), lambda g, s, tab: (tab[4, s], 0, g, 0, 0))]
    sd = jax.ShapeDtypeStruct
    out_shape = [sd((m, RWKV_INNER), F32), sd((m, RWKV_INNER), F32),
                 sd((n_seq, 2, RWKV_HEADS // 2, LANES, LANES), F32)]
    grid_spec = pltpu.PrefetchScalarGridSpec(
        num_scalar_prefetch=1, grid=(ng, steps), in_specs=in_specs, out_specs=out_specs,
        scratch_shapes=[pltpu.VMEM((2, pairs, LANES, LANES), F32)])
    args = list(ops) + [v, pc] + list(ops) + [v, pc] + [s0]
    return pl.pallas_call(
        functools.partial(_rwkv_scan_kernel, pairs=pairs), grid_spec=grid_spec, out_shape=out_shape,
        compiler_params=_cparams(("arbitrary", "arbitrary")), name="rwkv_scan",
    )(tables, *args)


def _rwkv_post_kernel(of_ref, ob_ref, bonus_ref, g_ref, lng_ref, lnb_ref, seg_ref, y_ref):
    o = of_ref[...] + ob_ref[...]
    seg = seg_ref[...]
    inv = 1.0 / HEAD_DIM
    nlb = o.shape[1] // LANES
    for j in range(nlb):
        sl = slice(j * LANES, (j + 1) * LANES)
        oj = o[:, sl]
        mu = _dot_const_rhs2(oj, seg) * inv
        cj = oj - mu
        var = _dot_const_rhs2(cj * cj, seg) * inv
        yj = cj * lax.rsqrt(var + RWKV_LN_EPS) * lng_ref[:, sl] + lnb_ref[:, sl]
        y_ref[:, sl] = ((yj + bonus_ref[:, sl]) * g_ref[:, sl]).astype(y_ref.dtype)


def _rwkv_post(o_f, o_b, bonus, g, ln_g, ln_b, seg, tm):
    m, w = o_f.shape
    tok = lambda: pl.BlockSpec((tm, w), lambda i: (i, 0))
    row = lambda: pl.BlockSpec((1, w), lambda i: (0, 0))
    return pl.pallas_call(
        _rwkv_post_kernel, grid=(m // tm,),
        in_specs=[tok(), tok(), tok(), tok(), row(), row(), pl.BlockSpec((LANES, LANES), lambda i: (0, 0))],
        out_specs=tok(), out_shape=jax.ShapeDtypeStruct((m, w), BF16),
        compiler_params=_cparams(("arbitrary",)), name="rwkv_post",
    )(o_f, o_b, bonus, g, ln_g, ln_b, seg)


def _seg_ones():
    i = np.arange(LANES)
    return jnp.asarray((i[:, None] // HEAD_DIM) == (i[None, :] // HEAD_DIM), BF16)


def _rwkv_consts(tm):
    i = np.arange(tm)
    same = (i[:, None] // RWKV_CHUNK) == (i[None, :] // RWKV_CHUNK)
    tril = same & (i[None, :] <= i[:, None])
    triu = same & (i[None, :] >= i[:, None])
    return jnp.asarray(tril, BF16), jnp.asarray(triu, BF16), jnp.asarray(same, BF16)


def _rwkv_layer_weights(rwkv_mu, w0, w2, a0, a2, g2, kkp, kap, rk, tm):
    inner = RWKV_INNER
    mu = rwkv_mu
    o_wl, o_al, o_gl = 3 * inner, 3 * inner + 128, 3 * inner + 256
    pad_g = 2 * LANES - RWKV_GATE_RANK

    def bd(w):
        z = jnp.zeros_like(w[0])
        return jnp.stack([jnp.concatenate([w[0], z], 0), jnp.concatenate([z, w[1]], 0)]).astype(BF16)

    tril, triu, ones = _rwkv_consts(tm)
    return {
        'mu_r': mu[:, 0:inner], 'mu_k': mu[:, inner:2 * inner], 'mu_v': mu[:, 2 * inner:3 * inner],
        'mu_wl': mu[:, o_wl:o_wl + 128], 'mu_al': mu[:, o_al:o_al + 128],
        'mu_gl': jnp.pad(mu[:, o_gl:], ((0, 0), (0, pad_g))),
        'w0': w0[:, None, :], 'w2': bd(w2), 'a0': a0[:, None, :], 'a2': bd(a2),
        'g2': jnp.pad(g2, ((0, pad_g), (0, 0))).astype(BF16),
        'kk': kkp[None, :], 'ka': kap[None, :], 'rk': rk.reshape(1, inner),
        'tril': tril, 'triu': triu, 'ones': ones, 'seg': _seg_ones(),
    }


def _scan_tables(n_ctx_seq, ctx_chunks, n_lat_seq, lat_chunks):
    fwd, bwd, first, last, seq = [], [], [], [], []
    base = 0
    sid = 0
    for n_seq, n in ((n_ctx_seq, ctx_chunks), (n_lat_seq, lat_chunks)):
        for q in range(n_seq):
            for c in range(n):
                fwd.append(base + c)
                bwd.append(base + n - 1 - c)
                first.append(int(c == 0))
                last.append(int(c == n - 1))
                seq.append(sid)
            base += n
            sid += 1
    return jnp.asarray(np.array([fwd, bwd, first, last, seq], np.int32))


def _pair_blockdiag(s):
    sh = s.shape[:-3]
    h = s.shape[-3]
    s = s.reshape(sh + (h // 2, 2, HEAD_DIM, HEAD_DIM))
    z = jnp.zeros_like(s[..., 0, :, :])
    top = jnp.concatenate([s[..., 0, :, :], z], axis=-1)
    bot = jnp.concatenate([z, s[..., 1, :, :]], axis=-1)
    return jnp.concatenate([top, bot], axis=-2)


def _pair_unblock(s):
    a = s[..., :HEAD_DIM, :HEAD_DIM]
    b = s[..., HEAD_DIM:, HEAD_DIM:]
    out = jnp.stack([a, b], axis=-3)
    return out.reshape(s.shape[:-3] + (2 * s.shape[-3], HEAD_DIM, HEAD_DIM))


def rwkv_mix(p, lw, s0_bd, tables, *, tm, n_ctx_tokens, ctx_len, lat_len, pairs):
    outs = _rwkv_prep(p, lw, tm=tm, n_ctx_tokens=n_ctx_tokens, ctx_len=ctx_len, lat_len=lat_len)
    ops, (v, bonus, g, pc) = outs[:6], outs[6:]
    o_f, o_b, s_fin = _rwkv_scan(ops, v, pc, s0_bd, tables, pairs=pairs)
    y = _rwkv_post(o_f, o_b, bonus, g, lw['ln_g'], lw['ln_b'], lw['seg'], tm)
    return y, s_fin


def _branch_merge_kernel(y0_ref, y1_ref, y2_ref, w_ref, g0_ref, g1_ref, g2_ref, o_ref):
    acc = _sigmoid(g0_ref[...]) * _dot(y0_ref[...], w_ref[0])
    acc += _sigmoid(g1_ref[...]) * _dot(y1_ref[...], w_ref[1])
    acc += _sigmoid(g2_ref[...]) * _dot(y2_ref[...], w_ref[2])
    o_ref[...] = acc.astype(o_ref.dtype)


def _branch_merge(ys, w_branch, p, tm, tn):
    m, kdim = ys[0].shape
    n = w_branch.shape[2]
    gb = COL_GATE // tn
    yspec = lambda: pl.BlockSpec((tm, kdim), lambda i, j: (i, 0))
    gspec = lambda b: pl.BlockSpec((tm, tn), lambda i, j: (i, gb + b * (n // tn) + j))
    return pl.pallas_call(
        _branch_merge_kernel, grid=(m // tm, n // tn),
        in_specs=[yspec(), yspec(), yspec(), pl.BlockSpec((N_BRANCHES, kdim, tn), lambda i, j: (0, 0, j)),
                  gspec(0), gspec(1), gspec(2)],
        out_specs=pl.BlockSpec((tm, tn), lambda i, j: (i, j)),
        out_shape=jax.ShapeDtypeStruct((m, n), BF16),
        compiler_params=_cparams(("arbitrary", "arbitrary")), name="branch_merge",
    )(ys[0], ys[1], ys[2], w_branch, p, p, p)


def _proj_residual_kernel(a_ref, w_ref, x_ref, mod_ref, o_ref, *, gate_row):
    g = mod_ref[0][gate_row:gate_row + 1, :]
    o_ref[...] = x_ref[...] + g * _dot(a_ref[...], w_ref[...])


def _proj_residual(a, w, x, mod, seg_tokens, gate_row, tm, tn):
    m, kdim = a.shape
    n = w.shape[1]
    return pl.pallas_call(
        functools.partial(_proj_residual_kernel, gate_row=gate_row), grid=(m // tm, n // tn),
        in_specs=[pl.BlockSpec((tm, kdim), lambda i, j: (i, 0)),
                  pl.BlockSpec((kdim, tn), lambda i, j: (0, j)),
                  pl.BlockSpec((tm, tn), lambda i, j: (i, j)),
                  pl.BlockSpec((1, SUBLANES, tn), lambda i, j: ((i * tm) // seg_tokens, 0, j))],
        out_specs=pl.BlockSpec((tm, tn), lambda i, j: (i, j)),
        out_shape=jax.ShapeDtypeStruct((m, n), F32),
        compiler_params=_cparams(("arbitrary", "arbitrary")), name="proj_residual",
    )(a, w, x, mod)


def _final_norm_kernel(x_ref, g_ref, o_ref):
    x = x_ref[...]
    o_ref[...] = x * lax.rsqrt(jnp.mean(x * x, axis=-1, keepdims=True) + NORM_EPS) * g_ref[...]


def _final_norm(x, g, tm):
    m, d = x.shape
    return pl.pallas_call(
        _final_norm_kernel, grid=(m // tm,),
        in_specs=[pl.BlockSpec((tm, d), lambda i: (i, 0)), pl.BlockSpec((1, d), lambda i: (0, 0))],
        out_specs=pl.BlockSpec((tm, d), lambda i: (i, 0)),
        out_shape=jax.ShapeDtypeStruct((m, d), F32),
        compiler_params=_cparams(("arbitrary",)), name="final_norm",
    )(x, g)


def _split_last(x, sizes):
    cuts = [int(s) for s in np.cumsum(sizes)[:-1]]
    return jnp.split(x, cuts, axis=-1)


def _centred_dwconv(x, w, b):
    width, ch = w.shape
    y = lax.conv_general_dilated(x, w[:, None, :].astype(x.dtype), window_strides=(1,),
                                 padding=[(width // 2, width // 2)],
                                 dimension_numbers=('NWC', 'WIO', 'NWC'), feature_group_count=ch)
    return y + b.astype(x.dtype)


def _flip(t, rev):
    return t[:, ::-1] if rev else t


def _segsum(a):
    t = a.shape[-1]
    cs = jnp.cumsum(a, axis=-1)
    diff = cs[..., :, None] - cs[..., None, :]
    return jnp.where(jnp.tril(jnp.ones((t, t), dtype=bool)), diff, -jnp.inf)


def _ssd_chunked(x, a, bm, cm, s0):
    b, t, h, p = x.shape
    n = bm.shape[-1]
    nc = t // SSD_CHUNK
    x = x.reshape(b, nc, SSD_CHUNK, h, p)
    bm = bm.reshape(b, nc, SSD_CHUNK, h, n)
    cm = cm.reshape(b, nc, SSD_CHUNK, h, n)
    a = a.reshape(b, nc, SSD_CHUNK, h).transpose(0, 3, 1, 2)
    a_cs = jnp.cumsum(a, axis=-1)
    cb = jnp.einsum('bclhn,bcshn->bhcls', cm, bm) * jnp.exp(_segsum(a))
    y_diag = jnp.einsum('bhcls,bcshp->bclhp', cb, x)
    decay_to_end = jnp.exp(a_cs[..., -1:] - a_cs).transpose(0, 2, 3, 1)
    states = jnp.einsum('bclhn,bclhp->bchpn', bm * decay_to_end[..., None], x)
    states = jnp.concatenate([s0[:, None], states], axis=1)
    chunk_decay = jnp.exp(_segsum(jnp.pad(a_cs[..., -1], ((0, 0), (0, 0), (1, 0)))))
    states = jnp.einsum('bhzc,bchpn->bzhpn', chunk_decay, states)
    prev_states, final = states[:, :-1], states[:, -1]
    decay_in = jnp.exp(a_cs).transpose(0, 2, 3, 1)
    y_off = jnp.einsum('bclhn,bchpn->bclhp', cm, prev_states) * decay_in[..., None]
    return (y_diag + y_off).reshape(b, t, h, p), final


def _ssd_branch_jax(z, xbc, dt_raw, lp, s0):
    b, t, _ = z.shape
    xbc = jax.nn.silu(_centred_dwconv(xbc, lp['ssd_conv_w'], lp['ssd_conv_b'])).astype(F32)
    xs, bm, cm = _split_last(xbc, (SSD_INNER, SSD_GROUPS * SSD_STATE, SSD_GROUPS * SSD_STATE))
    rep = SSD_HEADS // SSD_GROUPS
    xh = xs.reshape(b, t, SSD_HEADS, HEAD_DIM)
    bm = jnp.repeat(bm.reshape(b, t, SSD_GROUPS, SSD_STATE), rep, axis=2)
    cm = jnp.repeat(cm.reshape(b, t, SSD_GROUPS, SSD_STATE), rep, axis=2)
    dt = jax.nn.softplus(dt_raw.reshape(b, t, 2, SSD_HEADS) + lp['ssd_dt_bias'])
    a_neg = -jnp.exp(lp['ssd_a_log'])
    y = xh * lp['ssd_d'][:, None]
    finals = []
    for d in range(2):
        rev = d == 1
        yd, sd = _ssd_chunked(_flip(xh * dt[:, :, d, :, None], rev), _flip(dt[:, :, d] * a_neg[d], rev),
                              _flip(bm, rev), _flip(cm, rev), s0[:, d])
        y = y + _flip(yd, rev)
        finals.append(sd)
    y = y.reshape(b, t, SSD_INNER) * jax.nn.silu(z)
    yg = y.reshape(b, t, SSD_GROUPS, SSD_INNER // SSD_GROUPS)
    yg = yg * lax.rsqrt(jnp.mean(yg * yg, axis=-1, keepdims=True) + NORM_EPS)
    y = yg.reshape(b, t, SSD_INNER) * lp['ssd_norm_g']
    return y, jnp.stack(finals, axis=1)


def _context_attention_jax(q, k, v):
    scale = HEAD_DIM ** -0.5
    s = jnp.einsum('bqhd,bkhd->bhqk', q, k) * scale
    o = jnp.einsum('bhqk,bkhd->bqhd', jax.nn.softmax(s, axis=-1), v)
    return o.reshape(o.shape[0], o.shape[1], -1)


def _neighbourhood_attention_jax(q, k, v, k_ctx, v_ctx, rpb):
    b, t, h, d = q.shape
    rows = t // GRID_W
    wr = min(WIN_R, rows)
    n_win = wr * WIN_C
    r_ids = np.arange(rows)
    c_ids = np.arange(GRID_W)
    key_r = np.clip(r_ids - wr // 2, 0, rows - wr)[:, None] + np.arange(wr)
    key_c = np.clip(c_ids - WIN_C // 2, 0, GRID_W - WIN_C)[:, None] + np.arange(WIN_C)
    key_idx = (key_r[:, None, :, None] * GRID_W + key_c[None, :, None, :]).reshape(rows, GRID_W, n_win).astype(np.int32)
    off_r = (key_r - r_ids[:, None] + WIN_R - 1)[:, None, :, None]
    off_c = (key_c - c_ids[:, None] + WIN_C - 1)[None, :, None, :]
    bias = rpb[:, off_r, off_c].astype(F32)
    bias = bias.reshape(h, rows, GRID_W, n_win).transpose(1, 0, 2, 3)
    qb = q.reshape(b, rows, GRID_W, h, d).transpose(1, 0, 2, 3, 4)
    scale = HEAD_DIM ** -0.5

    def row_block(args):
        q_r, idx_r, bias_r = args
        kb = k[:, idx_r]
        vb = v[:, idx_r]
        s_win = jnp.einsum('bqhd,bqnhd->bhqn', q_r, kb) * scale + bias_r[None]
        s_ctx = jnp.einsum('bqhd,bchd->bhqc', q_r, k_ctx) * scale
        pr = jax.nn.softmax(jnp.concatenate([s_win, s_ctx], axis=-1), axis=-1)
        return (jnp.einsum('bhqn,bqnhd->bqhd', pr[..., :n_win], vb)
                + jnp.einsum('bhqc,bchd->bqhd', pr[..., n_win:], v_ctx))
    o = lax.map(row_block, (qb, jnp.asarray(key_idx), bias))
    return o.transpose(1, 0, 2, 3, 4).reshape(b, t, h * d)


def _moe_jax(xf, router_w, router_b, w_gu, b_gu, w_down, b_down):
    n_tok, dm = xf.shape
    blk = 128
    logits = jnp.dot(xf, router_w, precision=lax.Precision.HIGHEST) + router_b
    top_v, top_i = lax.top_k(logits, TOP_K)
    probs = jax.nn.softmax(top_v, axis=-1)
    flat_e = top_i.reshape(-1)
    order = jnp.argsort(flat_e)
    sorted_e = flat_e[order]
    tok = (order // TOP_K).astype(jnp.int32)
    wts = probs.reshape(-1)[order]
    counts = jnp.bincount(flat_e, length=N_EXPERTS)
    padded = (counts + blk - 1) // blk * blk
    pad_end = jnp.cumsum(padded)
    rank = jnp.arange(n_tok * TOP_K) - (jnp.cumsum(counts) - counts)[sorted_e]
    slot = (pad_end - padded)[sorted_e] + rank
    n_blocks = -(-(n_tok * TOP_K) // blk) + N_EXPERTS
    cap = n_blocks * blk
    slot_tok = jnp.full((cap,), n_tok, jnp.int32).at[slot].set(tok)
    slot_w = jnp.zeros((cap,), F32).at[slot].set(wts)
    block_e = jnp.minimum(jnp.searchsorted(pad_end, jnp.arange(n_blocks) * blk, side='right'), N_EXPERTS - 1)
    x_pad = jnp.concatenate([xf, jnp.zeros((1, dm), xf.dtype)], axis=0)
    xb = x_pad[slot_tok].reshape(n_blocks, blk, dm)

    def expert_block(args):
        xblk, e = args
        gate, up = jnp.split(xblk @ w_gu[e] + b_gu[e], 2, axis=-1)
        gate = jnp.minimum(gate, SWIGLU_LIMIT)
        up = jnp.clip(up, -SWIGLU_LIMIT, SWIGLU_LIMIT)
        act = gate * jax.nn.sigmoid(gate * SWIGLU_ALPHA) * (up + 1.0)
        return act @ w_down[e] + b_down[e]
    yb = lax.map(expert_block, (xb, block_e))
    y = jnp.zeros((n_tok + 1, dm), xf.dtype).at[slot_tok].add(yb.reshape(cap, dm) * slot_w[:, None])
    return y[:n_tok]


def _rmsnorm_jax(x, g):
    return x * lax.rsqrt(jnp.mean(x * x, axis=-1, keepdims=True) + NORM_EPS) * g


def _relayout_w_in(w):
    z, xbc, dt, rw, q, k, v, gate = _split_last(
        w, (SSD_INNER, SSD_XBC, 2 * SSD_HEADS, RWKV_COLS, NA_INNER, NA_INNER, NA_INNER, N_BRANCHES * D_MODEL))
    padc = lambda a, n: jnp.pad(a, ((0, 0), (0, n - a.shape[1])))
    parts = [z, xbc, padc(rw, RWKV_PAD), padc(dt, DT_PAD), q, k, padc(v, COL_GATE - COL_V), gate]
    out = jnp.concatenate(parts, axis=1)
    return padc(out, IN_COLS_PAD).astype(BF16)


def kernel(x_prompt, x_sample, cache_na_k, cache_na_v, state_ssd, state_rwkv, c, c_ctx, norm1_g, norm2_g, w_mod, b_mod, w_in, ssd_conv_w, ssd_conv_b, ssd_dt_bias, ssd_a_log, ssd_d, ssd_norm_g, rwkv_mu, rwkv_w0, rwkv_w2, rwkv_a0, rwkv_a2, rwkv_g2, rwkv_kk, rwkv_ka, rwkv_rk, rwkv_ln_g, rwkv_ln_b, na_rpb, w_branch, w_out, router_w, router_b, moe_w_gu, moe_b_gu, moe_w_down, moe_b_down, final_g):
    nb, seq, d = x_prompt.shape
    db, dseq, _ = x_sample.shape
    n_ctx = nb * seq
    n_lat = db * dseq
    assert n_ctx == dseq, "modulation segments assume context tokens fill exactly one latent-length segment"
    seg_tokens = dseq
    n_seg = 1 + db
    m = n_ctx + n_lat
    rwkv_tm = 256
    tables = _scan_tables(nb, seq // RWKV_CHUNK, db, dseq // RWKV_CHUNK)

    x = jnp.concatenate([x_prompt.reshape(n_ctx, d), x_sample.reshape(n_lat, d)], axis=0)
    cond = jnp.zeros((SUBLANES, d), F32).at[0].set(c_ctx).at[1:1 + db].set(c)

    ctx_k, ctx_v, ctx_ssd, ctx_rwkv = [], [], [], []
    for l in range(DEPTH):
        mod = _modulation(cond, w_mod[l], b_mod[l])
        mod = mod.reshape(SUBLANES, 6, d)[:n_seg]
        mod = jnp.pad(mod, ((0, 0), (0, SUBLANES - 6), (0, 0)))
        p = _norm_matmul(x, norm1_g[l][None], mod, _relayout_w_in(w_in[l]), seg_tokens, 0, 1, 1024, IN_TN)

        lw = _rwkv_layer_weights(rwkv_mu[l], rwkv_w0[l], rwkv_w2[l], rwkv_a0[l], rwkv_a2[l], rwkv_g2[l],
                                 rwkv_kk[l], rwkv_ka[l], rwkv_rk[l], rwkv_tm)
        lw['ln_g'] = rwkv_ln_g[l][None]
        lw['ln_b'] = rwkv_ln_b[l][None]
        s0 = jnp.concatenate([jnp.zeros((nb,) + state_rwkv.shape[2:], F32), state_rwkv[:, l]], axis=0)
        y_rwkv, s_rwkv = rwkv_mix(p, lw, _pair_blockdiag(s0), tables, tm=rwkv_tm, n_ctx_tokens=n_ctx,
                                  ctx_len=seq, lat_len=dseq, pairs=2)
        ctx_rwkv.append(_pair_unblock(s_rwkv[:nb]))

        lp = {'ssd_conv_w': ssd_conv_w[l], 'ssd_conv_b': ssd_conv_b[l], 'ssd_dt_bias': ssd_dt_bias[l],
              'ssd_a_log': ssd_a_log[l], 'ssd_d': ssd_d[l], 'ssd_norm_g': ssd_norm_g[l]}
        z = p[:, COL_Z:COL_Z + SSD_INNER]
        xbc = p[:, COL_XBC:COL_XBC + SSD_XBC]
        dt = p[:, COL_DT:COL_DT + 2 * SSD_HEADS]
        y_ssd_c, s_ssd_c = _ssd_branch_jax(z[:n_ctx].reshape(nb, seq, -1), xbc[:n_ctx].reshape(nb, seq, -1),
                                           dt[:n_ctx].reshape(nb, seq, -1), lp,
                                           jnp.zeros((nb,) + state_ssd.shape[2:], F32))
        y_ssd_l, _ = _ssd_branch_jax(z[n_ctx:].reshape(db, dseq, -1), xbc[n_ctx:].reshape(db, dseq, -1),
                                     dt[n_ctx:].reshape(db, dseq, -1), lp, state_ssd[:, l])
        y_ssd = jnp.concatenate([y_ssd_c.reshape(n_ctx, -1), y_ssd_l.reshape(n_lat, -1)], axis=0).astype(BF16)
        ctx_ssd.append(s_ssd_c)

        q = p[:, COL_Q:COL_Q + NA_INNER]
        k = p[:, COL_K:COL_K + NA_INNER]
        v = p[:, COL_V:COL_V + NA_INNER]
        hd = (NA_HEADS, HEAD_DIM)
        kc = k[:n_ctx].reshape((nb, seq) + hd)
        vc = v[:n_ctx].reshape((nb, seq) + hd)
        ctx_k.append(kc)
        ctx_v.append(vc)
        y_na_c = _context_attention_jax(q[:n_ctx].reshape((nb, seq) + hd), kc, vc)
        y_na_l = _neighbourhood_attention_jax(q[n_ctx:].reshape((db, dseq) + hd), k[n_ctx:].reshape((db, dseq) + hd),
                                              v[n_ctx:].reshape((db, dseq) + hd), cache_na_k[:, l], cache_na_v[:, l],
                                              na_rpb[l])
        y_na = jnp.concatenate([y_na_c.reshape(n_ctx, -1), y_na_l.reshape(n_lat, -1)], axis=0).astype(BF16)

        merged = _branch_merge((y_ssd, y_rwkv, y_na), w_branch[l].astype(BF16), p, 512, 512)
        x = _proj_residual(merged, w_out[l].astype(BF16), x, mod, seg_tokens, 2, 1024, 512)

        rep = lambda r: jnp.repeat(mod[:, r], seg_tokens, axis=0)
        hn = _rmsnorm_jax(x, norm2_g[l]) * (1.0 + rep(4)) + rep(3)
        x = x + rep(5) * _moe_jax(hn, router_w[l], router_b[l], moe_w_gu[l], moe_b_gu[l], moe_w_down[l], moe_b_down[l])

    y = _final_norm(x, final_g[None], 1024)
    y_prompt = y[:n_ctx].reshape(nb, seq, d)
    y_sample = y[n_ctx:].reshape(db, dseq, d)
    return (y_prompt, y_sample, jnp.stack(ctx_k, axis=1), jnp.stack(ctx_v, axis=1),
            jnp.stack(ctx_ssd, axis=1), jnp.stack(ctx_rwkv, axis=1))
```

```python
import functools
import math

import jax
import jax.numpy as jnp
import numpy as np
from jax import lax
from jax.experimental import pallas as pl
from jax.experimental.pallas import tpu as pltpu

F32 = jnp.float32
BF16 = jnp.bfloat16

D_MODEL = 2048
DEPTH = 4
GRID_W = 64
HEAD_DIM = 64
MIX_WIDTH = D_MODEL // 2
N_BRANCHES = 3
SSD_HEADS = MIX_WIDTH // HEAD_DIM
SSD_INNER = MIX_WIDTH
SSD_GROUPS = 2
SSD_STATE = 128
SSD_CHUNK = 128
SSD_CONV = 3
SSD_XBC = SSD_INNER + 2 * SSD_GROUPS * SSD_STATE
RWKV_HEADS = MIX_WIDTH // HEAD_DIM
RWKV_INNER = MIX_WIDTH
RWKV_DECAY_RANK = 64
RWKV_ICLR_RANK = 64
RWKV_GATE_RANK = 160
RWKV_COLS = 3 * RWKV_INNER + 2 * RWKV_DECAY_RANK + 2 * RWKV_ICLR_RANK + RWKV_GATE_RANK
RWKV_LN_EPS = 64e-5
NA_HEADS = MIX_WIDTH // HEAD_DIM
NA_INNER = MIX_WIDTH
WIN_R = 8
WIN_C = 16
N_EXPERTS = 32
TOP_K = 4
D_FF = D_MODEL
SWIGLU_LIMIT = 7.0
SWIGLU_ALPHA = 1.702
NORM_EPS = 1e-6

LANES = 128
SUBLANES = 8
VMEM_LIMIT = 56 * 1024 * 1024

RWKV_PAD = 3584
DT_PAD = LANES
COL_Z = 0
COL_RWKV = COL_Z + SSD_INNER
COL_XBC = COL_RWKV + RWKV_PAD
COL_DT = COL_XBC + SSD_XBC
COL_Q = COL_DT + DT_PAD
COL_K = COL_Q + NA_INNER
COL_V = COL_K + NA_INNER
IN_TN = 512
COL_QKV_END = COL_V + NA_INNER
COL_GATE = -(-COL_QKV_END // IN_TN) * IN_TN
IN_COLS_USED = COL_GATE + N_BRANCHES * D_MODEL
IN_COLS_PAD = -(-IN_COLS_USED // IN_TN) * IN_TN

RWKV_CHUNK = 64
DECAY_SCALE = math.exp(-0.5)


def _cparams(sem):
    return pltpu.CompilerParams(dimension_semantics=sem, vmem_limit_bytes=VMEM_LIMIT)


def _dot(a, b):
    return jnp.dot(a, b, preferred_element_type=F32)


def _dot_nt(a, b):
    return lax.dot_general(a, b, (((1,), (1,)), ((), ())), preferred_element_type=F32)


def _split3(x):
    hi = x.astype(BF16)
    r1 = x - hi.astype(F32)
    mid = r1.astype(BF16)
    lo = (r1 - mid.astype(F32)).astype(BF16)
    return hi, mid, lo


def _dot_const_lhs3(c_bf16, x):
    hi, mid, lo = _split3(x)
    return _dot(c_bf16, hi) + _dot(c_bf16, mid) + _dot(c_bf16, lo)


def _dot_const_rhs2(x, c_bf16):
    hi = x.astype(BF16)
    lo = (x - hi.astype(F32)).astype(BF16)
    return _dot(hi, c_bf16) + _dot(lo, c_bf16)


def _sigmoid(x):
    return 1.0 / (1.0 + jnp.exp(-x))


def _silu(x):
    return x * _sigmoid(x)


def _mod_kernel(c_ref, w_ref, b_ref, o_ref):
    a = _silu(c_ref[...]).astype(BF16)
    o_ref[...] = _dot(a, w_ref[...].astype(BF16)) + b_ref[...]


def _modulation(cond, w_mod, b_mod):
    n = w_mod.shape[1]
    tn = 1024
    return pl.pallas_call(
        _mod_kernel,
        grid=(n // tn,),
        in_specs=[pl.BlockSpec((SUBLANES, D_MODEL), lambda j: (0, 0)),
                  pl.BlockSpec((D_MODEL, tn), lambda j: (0, j)),
                  pl.BlockSpec((1, tn), lambda j: (0, j))],
        out_specs=pl.BlockSpec((SUBLANES, tn), lambda j: (0, j)),
        out_shape=jax.ShapeDtypeStruct((SUBLANES, n), F32),
        compiler_params=_cparams(("arbitrary",)),
        name="adaln_modulation",
    )(cond, w_mod, b_mod.reshape(1, n))


def _norm_matmul_kernel(x_ref, g_ref, mod_ref, w_ref, o_ref, h_ref, *, shift_row, scale_row):
    @pl.when(pl.program_id(1) == 0)
    def _():
        x = x_ref[...]
        y = x * lax.rsqrt(jnp.mean(x * x, axis=-1, keepdims=True) + NORM_EPS)
        y = y * g_ref[...]
        m = mod_ref[0]
        y = y * (1.0 + m[scale_row:scale_row + 1, :]) + m[shift_row:shift_row + 1, :]
        h_ref[...] = y.astype(BF16)

    o_ref[...] = _dot(h_ref[...], w_ref[...]).astype(o_ref.dtype)


def _norm_matmul(x, g, mod, w, seg_tokens, shift_row, scale_row, tm, tn, out_dtype=F32):
    m, d = x.shape
    n = w.shape[1]
    kern = functools.partial(_norm_matmul_kernel, shift_row=shift_row, scale_row=scale_row)
    return pl.pallas_call(
        kern,
        grid=(m // tm, n // tn),
        in_specs=[pl.BlockSpec((tm, d), lambda i, j: (i, 0)),
                  pl.BlockSpec((1, d), lambda i, j: (0, 0)),
                  pl.BlockSpec((1, SUBLANES, d), lambda i, j: ((i * tm) // seg_tokens, 0, 0)),
                  pl.BlockSpec((d, tn), lambda i, j: (0, j))],
        out_specs=pl.BlockSpec((tm, tn), lambda i, j: (i, j)),
        out_shape=jax.ShapeDtypeStruct((m, n), out_dtype),
        scratch_shapes=[pltpu.VMEM((tm, d), BF16)],
        compiler_params=_cparams(("arbitrary", "arbitrary")),
        name="norm_matmul",
    )(x, g, mod, w)


def _shifted(x, prev8, next8, mu_p, mu_n, has_prev, has_next):
    tm = x.shape[0]
    rows = lax.broadcasted_iota(jnp.int32, x.shape, 0)
    p_row = jnp.where(has_prev, prev8[SUBLANES - 1:SUBLANES, :], 0.0)
    n_row = jnp.where(has_next, next8[0:1, :], 0.0)
    x_prev = jnp.where(rows == 0, p_row, pltpu.roll(x, 1, axis=0))
    x_next = jnp.where(rows == tm - 1, n_row, pltpu.roll(x, tm - 1, axis=0))
    return x + mu_p * (x_prev - x) + mu_n * (x_next - x)


def _rwkv_prep_kernel(
        r_ref, r_p, r_n, k_ref, k_p, k_n, v_ref, v_p, v_n,
        wl_ref, wl_p, wl_n, al_ref, al_p, al_n, gl_ref, gl_p, gl_n,
        mu_r, mu_k, mu_v, mu_wl, mu_al, mu_gl,
        w0_ref, w2_ref, a0_ref, a2_ref, g2_ref, kkp_ref, kap_ref, rk_ref,
        tril_ref, triu_ref, ones_ref, seg_ref,
        alpha_o, beta_o, kappa_o, rho_o, kappa_p_o, beta_p_o, v_o, bonus_o, g_o, pc_o,
        *, tm, n_ctx_tokens, ctx_len, lat_len):
    i = pl.program_id(0)
    tok0 = i * tm
    in_ctx = tok0 < n_ctx_tokens
    pos = jnp.where(in_ctx, tok0 % ctx_len, (tok0 - n_ctx_tokens) % lat_len)
    seq_len = jnp.where(in_ctx, ctx_len, lat_len)
    has_prev = pos != 0
    has_next = pos + tm != seq_len

    def sh(ref, p, n, mu):
        return _shifted(ref[...], p[...], n[...], mu[0:1, :], mu[1:2, :], has_prev, has_next)

    r = sh(r_ref, r_p, r_n, mu_r)
    k = sh(k_ref, k_p, k_n, mu_k)
    v = sh(v_ref, v_p, v_n, mu_v)
    w_lo = sh(wl_ref, wl_p, wl_n, mu_wl)
    a_lo = sh(al_ref, al_p, al_n, mu_al)
    g_lo = sh(gl_ref, gl_p, gl_n, mu_gl)

    seg = seg_ref[...]
    kk = k * kkp_ref[...]
    ss = _dot_const_rhs2(kk * kk, seg)
    kk = kk / jnp.maximum(jnp.sqrt(ss), 1e-12)

    tw = jnp.tanh(w_lo).astype(BF16)
    ab = a_lo.astype(BF16)
    g_o[...] = _dot(_sigmoid(g_lo).astype(BF16), g2_ref[...])
    v_o[...] = v.astype(BF16)

    tri = (tril_ref[...], triu_ref[...])
    ones_blk = ones_ref[...]
    kap = kap_ref[...]
    k_sum = jnp.zeros_like(k)
    nch = tm // RWKV_CHUNK
    for d in range(2):
        logw = w0_ref[d] + _dot(tw, w2_ref[d])
        lw = -DECAY_SCALE * _sigmoid(logw)
        a = _sigmoid(a0_ref[d] + _dot(ab, a2_ref[d]))
        b = kk * a
        k_d = k * (1.0 + (a - 1.0) * kap)
        k_sum = k_sum + k_d
        cum = _dot_const_lhs3(tri[d], lw)
        tot = _dot_const_lhs3(ones_blk, lw)
        e_neg = jnp.exp(-cum)
        e_rest = jnp.exp(tot - cum)
        alpha_o[d] = (kk * jnp.exp(cum - lw)).astype(BF16)
        beta_o[d] = (b * e_neg).astype(BF16)
        kappa_o[d] = (k_d * e_neg).astype(BF16)
        rho_o[d] = (r * jnp.exp(cum)).astype(BF16)
        kappa_p_o[d] = (k_d * e_rest).astype(BF16)
        beta_p_o[d] = (b * e_rest).astype(BF16)
        pcs = jnp.exp(tot)
        for c in range(nch):
            pc_o[d, c] = pcs[c * RWKV_CHUNK:c * RWKV_CHUNK + 1, :]
    bonus_o[...] = _dot_const_rhs2(r * k_sum * rk_ref[...], seg) * v


def _rwkv_prep(p, lw, *, tm, n_ctx_tokens, ctx_len, lat_len):
    m = p.shape[0]
    nlb = RWKV_INNER // LANES
    hb = tm // SUBLANES
    last8 = m // SUBLANES - 1
    cb = COL_RWKV // LANES

    def tile(col_blk, width=LANES, per_j=True):
        wb = width // LANES
        if per_j:
            cm = lambda i, j: (i, col_blk + j)
            cp = lambda i, j: (jnp.maximum(i * hb - 1, 0), col_blk + j)
            cn = lambda i, j: (jnp.minimum((i + 1) * hb, last8), col_blk + j)
        else:
            cm = lambda i, j: (i, col_blk // wb)
            cp = lambda i, j: (jnp.maximum(i * hb - 1, 0), col_blk // wb)
            cn = lambda i, j: (jnp.minimum((i + 1) * hb, last8), col_blk // wb)
        return [pl.BlockSpec((tm, width), cm), pl.BlockSpec((SUBLANES, width), cp),
                pl.BlockSpec((SUBLANES, width), cn)]

    gw = 2 * LANES
    in_specs = (tile(cb) + tile(cb + nlb) + tile(cb + 2 * nlb)
                + tile(cb + 3 * nlb, per_j=False) + tile(cb + 3 * nlb + 1, per_j=False)
                + tile(cb + 3 * nlb + 2, width=gw, per_j=False))
    per_lane = lambda rows: pl.BlockSpec((rows, LANES), lambda i, j: (0, j))
    const2 = lambda a, b: pl.BlockSpec((a, b), lambda i, j: (0, 0))
    in_specs += [per_lane(2), per_lane(2), per_lane(2), const2(2, LANES), const2(2, LANES), const2(2, gw)]
    in_specs += [pl.BlockSpec((2, 1, LANES), lambda i, j: (0, 0, j)),
                 pl.BlockSpec((2, LANES, LANES), lambda i, j: (0, 0, j)),
                 pl.BlockSpec((2, 1, LANES), lambda i, j: (0, 0, j)),
                 pl.BlockSpec((2, LANES, LANES), lambda i, j: (0, 0, j)),
                 pl.BlockSpec((gw, LANES), lambda i, j: (0, j)),
                 per_lane(1), per_lane(1), per_lane(1),
                 const2(tm, tm), const2(tm, tm), const2(tm, tm), const2(LANES, LANES)]
    tok_d = lambda: pl.BlockSpec((2, tm, LANES), lambda i, j: (0, i, j))
    tok = lambda: pl.BlockSpec((tm, LANES), lambda i, j: (i, j))
    nch = tm // RWKV_CHUNK
    out_specs = [tok_d() for _ in range(6)] + [tok(), tok(), tok(),
                 pl.BlockSpec((2, nch, 1, LANES), lambda i, j: (0, i, 0, j))]
    sd = jax.ShapeDtypeStruct
    out_shape = [sd((2, m, RWKV_INNER), BF16) for _ in range(6)] + [
        sd((m, RWKV_INNER), BF16), sd((m, RWKV_INNER), F32), sd((m, RWKV_INNER), F32),
        sd((2, m // RWKV_CHUNK, 1, RWKV_INNER), F32)]
    kern = functools.partial(_rwkv_prep_kernel, tm=tm, n_ctx_tokens=n_ctx_tokens, ctx_len=ctx_len, lat_len=lat_len)
    args = [p] * 18 + [lw['mu_r'], lw['mu_k'], lw['mu_v'], lw['mu_wl'], lw['mu_al'], lw['mu_gl'],
                       lw['w0'], lw['w2'], lw['a0'], lw['a2'], lw['g2'], lw['kk'], lw['ka'], lw['rk'],
                       lw['tril'], lw['triu'], lw['ones'], lw['seg']]
    return pl.pallas_call(
        kern, grid=(m // tm, nlb), in_specs=in_specs, out_specs=out_specs, out_shape=out_shape,
        compiler_params=_cparams(("arbitrary", "arbitrary")), name="rwkv_prep",
    )(*args)


def _rwkv_scan_kernel(tab_ref,
                      al_f, be_f, ka_f, rh_f, kp_f, bp_f, v_f, pc_f,
                      al_b, be_b, ka_b, rh_b, kp_b, bp_b, v_b, pc_b,
                      s0_ref, of_ref, ob_ref, sfin_ref, s_scr, *, pairs):
    s = pl.program_id(1)
    is_first = tab_ref[2, s] == 1
    is_last = tab_ref[3, s] == 1

    @pl.when(is_first)
    def _():
        s_scr[...] = s0_ref[0]

    c2 = 2 * RWKV_CHUNK
    lane = lax.broadcasted_iota(jnp.int32, (1, LANES), 1)
    m0 = lane < HEAD_DIM
    ri = lax.broadcasted_iota(jnp.int32, (c2, c2), 0)
    ci = lax.broadcasted_iota(jnp.int32, (c2, c2), 1)
    eye = (ri == ci).astype(F32)
    same_head = (ri // HEAD_DIM) == (ci // HEAD_DIM)
    zero_b = jnp.zeros((), BF16)

    def stack(x):
        z = jnp.zeros_like(x)
        return jnp.concatenate([jnp.where(m0, x, z), jnp.where(m0, z, x)], axis=0)

    def fold(x):
        return x[:RWKV_CHUNK] + x[RWKV_CHUNK:]

    dirs = ((al_f, be_f, ka_f, rh_f, kp_f, bp_f, v_f, pc_f, of_ref, ci < ri, ci <= ri),
            (al_b, be_b, ka_b, rh_b, kp_b, bp_b, v_b, pc_b, ob_ref, ci > ri, ci >= ri))
    for d, (al_r, be_r, ka_r, rh_r, kp_r, bp_r, v_r, pc_r, o_r, strict, incl) in enumerate(dirs):
        for g in range(pairs):
            sl = slice(g * LANES, (g + 1) * LANES)
            al, be, ka, rh = al_r[0, :, sl], be_r[0, :, sl], ka_r[0, :, sl], rh_r[0, :, sl]
            kp, bp, v = kp_r[0, :, sl], bp_r[0, :, sl], v_r[:, sl]
            pc = pc_r[0, 0, :, sl]
            s_bd = s_scr[d, g]
            lhs = jnp.concatenate([stack(al), stack(rh)], axis=0)
            sb = _dot_nt(lhs, stack(be))
            sk = _dot_nt(lhs, stack(ka))
            l_b = jnp.where(strict, sb[:c2], 0.0)
            l_k = jnp.where(strict, sk[:c2], 0.0)
            a_b = jnp.where(incl, sb[c2:], 0.0)
            a_k = jnp.where(incl, sk[c2:], 0.0)
            x = -l_b
            t = eye + x
            for _ in range(5):
                xb = x.astype(BF16)
                x = _dot(xb, xb)
                t = t + _dot(t.astype(BF16), x.astype(BF16))
            m2 = _dot_nt(jnp.concatenate([al, rh], axis=0), s_bd.astype(BF16))
            lk_ak = jnp.concatenate([fold(l_k), fold(a_k)], axis=0).astype(BF16)
            m3 = _dot(lk_ak, stack(v))
            w1 = m2[:RWKV_CHUNK] + m3[:RWKV_CHUNK]
            u = fold(_dot(t.astype(BF16), stack(w1.astype(BF16))))
            ub = u.astype(BF16)
            o = m2[RWKV_CHUNK:] + m3[RWKV_CHUNK:] - _dot(fold(a_b).astype(BF16), stack(ub))
            o_r[:, sl] = o
            vu_t = jnp.concatenate([v.astype(F32), u], axis=0).T.astype(BF16)
            upd = _dot(vu_t, jnp.concatenate([kp, -bp], axis=0))
            s_scr[d, g] = s_bd * pc + jnp.where(same_head, upd, 0.0)

    @pl.when(is_last)
    def _():
        sfin_ref[0] = s_scr[...]


def _rwkv_scan(ops, v, pc, s0, tables, *, pairs):
    m = v.shape[0]
    n_seq = s0.shape[0]
    steps = tables.shape[1]
    gl = pairs * LANES
    ng = RWKV_INNER // gl

    def dspec(d, row):
        return pl.BlockSpec((1, RWKV_CHUNK, gl), lambda g, s, tab: (d, tab[row, s], g))

    def dir_specs(d, row):
        return [dspec(d, row) for _ in range(6)] + [
            pl.BlockSpec((RWKV_CHUNK, gl), lambda g, s, tab: (tab[row, s], g)),
            pl.BlockSpec((1, 1, 1, gl), lambda g, s, tab: (d, tab[row, s], 0, g))]

    in_specs = dir_specs(0, 0) + dir_specs(1, 1) + [
        pl.BlockSpec((1, 2, pairs, LANES, LANES), lambda g, s, tab: (tab[4, s], 0, g, 0, 0))]
    out_specs = [pl.BlockSpec((RWKV_CHUNK, gl), lambda g, s, tab: (tab[0, s], g)),
                 pl.BlockSpec((RWKV_CHUNK, gl), lambda g, s, tab: (tab[1, s], g)),
                 pl.BlockSpec((1, 2, pairs, LANES, LANES), lambda g, s, tab: (tab[4, s], 0, g, 0, 0))]
    sd = jax.ShapeDtypeStruct
    out_shape = [sd((m, RWKV_INNER), F32), sd((m, RWKV_INNER), F32),
                 sd((n_seq, 2, RWKV_HEADS // 2, LANES, LANES), F32)]
    grid_spec = pltpu.PrefetchScalarGridSpec(
        num_scalar_prefetch=1, grid=(ng, steps), in_specs=in_specs, out_specs=out_specs,
        scratch_shapes=[pltpu.VMEM((2, pairs, LANES, LANES), F32)])
    args = list(ops) + [v, pc] + list(ops) + [v, pc] + [s0]
    return pl.pallas_call(
        functools.partial(_rwkv_scan_kernel, pairs=pairs), grid_spec=grid_spec, out_shape=out_shape,
        compiler_params=_cparams(("arbitrary", "arbitrary")), name="rwkv_scan",
    )(tables, *args)


def _rwkv_post_kernel(of_ref, ob_ref, bonus_ref, g_ref, lng_ref, lnb_ref, seg_ref, y_ref):
    o = of_ref[...] + ob_ref[...]
    seg = seg_ref[...]
    inv = 1.0 / HEAD_DIM
    nlb = o.shape[1] // LANES
    for j in range(nlb):
        sl = slice(j * LANES, (j + 1) * LANES)
        oj = o[:, sl]
        mu = _dot_const_rhs2(oj, seg) * inv
        cj = oj - mu
        var = _dot_const_rhs2(cj * cj, seg) * inv
        yj = cj * lax.rsqrt(var + RWKV_LN_EPS) * lng_ref[:, sl] + lnb_ref[:, sl]
        y_ref[:, sl] = ((yj + bonus_ref[:, sl]) * g_ref[:, sl]).astype(y_ref.dtype)


def _rwkv_post(o_f, o_b, bonus, g, ln_g, ln_b, seg, tm):
    m, w = o_f.shape
    tok = lambda: pl.BlockSpec((tm, w), lambda i: (i, 0))
    row = lambda: pl.BlockSpec((1, w), lambda i: (0, 0))
    return pl.pallas_call(
        _rwkv_post_kernel, grid=(m // tm,),
        in_specs=[tok(), tok(), tok(), tok(), row(), row(), pl.BlockSpec((LANES, LANES), lambda i: (0, 0))],
        out_specs=tok(), out_shape=jax.ShapeDtypeStruct((m, w), BF16),
        compiler_params=_cparams(("arbitrary",)), name="rwkv_post",
    )(o_f, o_b, bonus, g, ln_g, ln_b, seg)


def _seg_ones():
    i = np.arange(LANES)
    return jnp.asarray((i[:, None] // HEAD_DIM) == (i[None, :] // HEAD_DIM), BF16)


def _rwkv_consts(tm):
    i = np.arange(tm)
    same = (i[:, None] // RWKV_CHUNK) == (i[None, :] // RWKV_CHUNK)
    tril = same & (i[None, :] <= i[:, None])
    triu = same & (i[None, :] >= i[:, None])
    return jnp.asarray(tril, BF16), jnp.asarray(triu, BF16), jnp.asarray(same, BF16)


def _rwkv_layer_weights(rwkv_mu, w0, w2, a0, a2, g2, kkp, kap, rk, tm):
    inner = RWKV_INNER
    mu = rwkv_mu
    o_wl, o_al, o_gl = 3 * inner, 3 * inner + 128, 3 * inner + 256
    pad_g = 2 * LANES - RWKV_GATE_RANK

    def bd(w):
        z = jnp.zeros_like(w[0])
        return jnp.stack([jnp.concatenate([w[0], z], 0), jnp.concatenate([z, w[1]], 0)]).astype(BF16)

    tril, triu, ones = _rwkv_consts(tm)
    return {
        'mu_r': mu[:, 0:inner], 'mu_k': mu[:, inner:2 * inner], 'mu_v': mu[:, 2 * inner:3 * inner],
        'mu_wl': mu[:, o_wl:o_wl + 128], 'mu_al': mu[:, o_al:o_al + 128],
        'mu_gl': jnp.pad(mu[:, o_gl:], ((0, 0), (0, pad_g))),
        'w0': w0[:, None, :], 'w2': bd(w2), 'a0': a0[:, None, :], 'a2': bd(a2),
        'g2': jnp.pad(g2, ((0, pad_g), (0, 0))).astype(BF16),
        'kk': kkp[None, :], 'ka': kap[None, :], 'rk': rk.reshape(1, inner),
        'tril': tril, 'triu': triu, 'ones': ones, 'seg': _seg_ones(),
    }


def _scan_tables(n_ctx_seq, ctx_chunks, n_lat_seq, lat_chunks):
    fwd, bwd, first, last, seq = [], [], [], [], []
    base = 0
    sid = 0
    for n_seq, n in ((n_ctx_seq, ctx_chunks), (n_lat_seq, lat_chunks)):
        for q in range(n_seq):
            for c in range(n):
                fwd.append(base + c)
                bwd.append(base + n - 1 - c)
                first.append(int(c == 0))
                last.append(int(c == n - 1))
                seq.append(sid)
            base += n
            sid += 1
    return jnp.asarray(np.array([fwd, bwd, first, last, seq], np.int32))


def _pair_blockdiag(s):
    sh = s.shape[:-3]
    h = s.shape[-3]
    s = s.reshape(sh + (h // 2, 2, HEAD_DIM, HEAD_DIM))
    z = jnp.zeros_like(s[..., 0, :, :])
    top = jnp.concatenate([s[..., 0, :, :], z], axis=-1)
    bot = jnp.concatenate([z, s[..., 1, :, :]], axis=-1)
    return jnp.concatenate([top, bot], axis=-2)


def _pair_unblock(s):
    a = s[..., :HEAD_DIM, :HEAD_DIM]
    b = s[..., HEAD_DIM:, HEAD_DIM:]
    out = jnp.stack([a, b], axis=-3)
    return out.reshape(s.shape[:-3] + (2 * s.shape[-3], HEAD_DIM, HEAD_DIM))


def rwkv_mix(p, lw, s0_bd, tables, *, tm, n_ctx_tokens, ctx_len, lat_len, pairs):
    outs = _rwkv_prep(p, lw, tm=tm, n_ctx_tokens=n_ctx_tokens, ctx_len=ctx_len, lat_len=lat_len)
    ops, (v, bonus, g, pc) = outs[:6], outs[6:]
    o_f, o_b, s_fin = _rwkv_scan(ops, v, pc, s0_bd, tables, pairs=pairs)
    y = _rwkv_post(o_f, o_b, bonus, g, lw['ln_g'], lw['ln_b'], lw['seg'], tm)
    return y, s_fin


SSD_BC = SSD_GROUPS * SSD_STATE


def _softplus(x):
    return jnp.maximum(x, 0.0) + jnp.log(1.0 + jnp.exp(-jnp.abs(x)))


def _ssd_prep_kernel(x_ref, x_p, x_n, dt_ref, w_ref, b_ref, dtb_ref, aneg_ref, act_o, dt_o, a_o,
                     *, tm, n_ctx_tokens, ctx_len, lat_len):
    i = pl.program_id(0)
    tok0 = i * tm
    in_ctx = tok0 < n_ctx_tokens
    pos = jnp.where(in_ctx, tok0 % ctx_len, (tok0 - n_ctx_tokens) % lat_len)
    seq_len = jnp.where(in_ctx, ctx_len, lat_len)
    has_prev = pos != 0
    has_next = pos + tm != seq_len
    x = x_ref[...]
    rows = lax.broadcasted_iota(jnp.int32, x.shape, 0)
    p_row = jnp.where(has_prev, x_p[SUBLANES - 1:SUBLANES, :], 0.0)
    n_row = jnp.where(has_next, x_n[0:1, :], 0.0)
    x_prev = jnp.where(rows == 0, p_row, pltpu.roll(x, 1, axis=0))
    x_next = jnp.where(rows == tm - 1, n_row, pltpu.roll(x, tm - 1, axis=0))
    y = x_prev * w_ref[0:1, :] + x * w_ref[1:2, :] + x_next * w_ref[2:3, :] + b_ref[...]
    act_o[...] = _silu(y).astype(act_o.dtype)
    dt = _softplus(dt_ref[...] + dtb_ref[...])
    dt_o[...] = dt
    a_o[...] = dt * aneg_ref[...]


def _ssd_prep(p, conv_w, conv_b, dt_bias, a_neg, *, tm, n_ctx_tokens, ctx_len, lat_len):
    m = p.shape[0]
    hb = tm // SUBLANES
    last8 = m // SUBLANES - 1
    wx = SSD_XBC
    cb = COL_XBC // wx if COL_XBC % wx == 0 else None
    assert cb is not None
    pad = lambda v: jnp.pad(v.reshape(1, -1), ((0, 0), (0, DT_PAD - v.size)))
    kern = functools.partial(_ssd_prep_kernel, tm=tm, n_ctx_tokens=n_ctx_tokens, ctx_len=ctx_len, lat_len=lat_len)
    sd = jax.ShapeDtypeStruct
    return pl.pallas_call(
        kern, grid=(m // tm,),
        in_specs=[pl.BlockSpec((tm, wx), lambda i: (i, cb)),
                  pl.BlockSpec((SUBLANES, wx), lambda i: (jnp.maximum(i * hb - 1, 0), cb)),
                  pl.BlockSpec((SUBLANES, wx), lambda i: (jnp.minimum((i + 1) * hb, last8), cb)),
                  pl.BlockSpec((tm, DT_PAD), lambda i: (i, COL_DT // DT_PAD)),
                  pl.BlockSpec((SSD_CONV, wx), lambda i: (0, 0)), pl.BlockSpec((1, wx), lambda i: (0, 0)),
                  pl.BlockSpec((1, DT_PAD), lambda i: (0, 0)), pl.BlockSpec((1, DT_PAD), lambda i: (0, 0))],
        out_specs=[pl.BlockSpec((tm, wx), lambda i: (i, 0)), pl.BlockSpec((tm, DT_PAD), lambda i: (i, 0)),
                   pl.BlockSpec((tm, DT_PAD), lambda i: (i, 0))],
        out_shape=[sd((m, wx), BF16), sd((m, DT_PAD), F32), sd((m, DT_PAD), F32)],
        compiler_params=_cparams(("arbitrary",)), name="ssd_prep",
    )(p, p, p, p, conv_w, conv_b.reshape(1, wx), pad(dt_bias), pad(a_neg))


def _ssd_scan_kernel(tab_ref, x_f, b_f, c_f, dt_f, a_f, x_b, b_b, c_b, dt_b, a_b, tril_ref, triu_ref,
                     s0_ref, yf_ref, yb_ref, sfin_ref, s_scr):
    s = pl.program_id(0)

    @pl.when(tab_ref[2, s] == 1)
    def _():
        s_scr[...] = s0_ref[0]

    c = SSD_CHUNK
    lane0 = lax.broadcasted_iota(jnp.int32, (1, LANES), 1) < HEAD_DIM
    ri = lax.broadcasted_iota(jnp.int32, (c, c), 0)
    ci = lax.broadcasted_iota(jnp.int32, (c, c), 1)
    npair = SSD_HEADS // 2
    hpg = SSD_HEADS // SSD_GROUPS
    dirs = ((x_f, b_f, c_f, dt_f, a_f, tril_ref, yf_ref, ci <= ri),
            (x_b, b_b, c_b, dt_b, a_b, triu_ref, yb_ref, ci >= ri))
    for d, (x_r, b_r, c_r, dt_r, a_r, tri_r, y_r, causal) in enumerate(dirs):
        cum = _dot_const_lhs3(tri_r[...], a_r[...])
        tot = cum[c - 1:c, :] if d == 0 else cum[0:1, :]
        cum_t = cum.T
        dt_t = dt_r[...].T
        gs = []
        for g in range(SSD_GROUPS):
            gl = slice(g * SSD_STATE, (g + 1) * SSD_STATE)
            gs.append(_dot_nt(c_r[:, gl], b_r[:, gl]))
        for pr in range(npair):
            sl = slice(pr * LANES, (pr + 1) * LANES)
            grp = (2 * pr) // hpg
            gl = slice(grp * SSD_STATE, (grp + 1) * SSD_STATE)
            xp = x_r[:, sl]
            ms, e_in, wgt, dec = [], [], [], []
            for hh in range(2):
                col = d * SSD_HEADS + 2 * pr + hh
                cc = cum[:, col:col + 1]
                diff = jnp.where(causal, cc - cum_t[col:col + 1, :], MASK_BIAS)
                ms.append((gs[grp] * jnp.exp(diff) * dt_t[col:col + 1, :]).astype(BF16))
                e_in.append(jnp.exp(cc))
                wgt.append(jnp.exp(tot[:, col:col + 1] - cc) * dt_r[:, col:col + 1])
                dec.append(jnp.exp(tot[:, col:col + 1]))
            z = jnp.zeros_like(xp)
            x_stack = jnp.concatenate([jnp.where(lane0, xp, z), jnp.where(lane0, z, xp)], axis=0)
            y_diag = _dot(jnp.concatenate(ms, axis=1), x_stack)
            s_pair = s_scr[d, pr]
            y_off = _dot_nt(c_r[:, gl], s_pair.astype(BF16)) * jnp.where(lane0, e_in[0], e_in[1])
            y_r[:, sl] = y_diag + y_off
            xw = xp.astype(F32) * jnp.where(lane0, wgt[0], wgt[1])
            upd = _dot(xw.T.astype(BF16), b_r[:, gl])
            rows_h0 = lax.broadcasted_iota(jnp.int32, (LANES, 1), 0) < HEAD_DIM
            s_scr[d, pr] = s_pair * jnp.where(rows_h0, dec[0], dec[1]) + upd

    @pl.when(tab_ref[3, s] == 1)
    def _():
        sfin_ref[0] = s_scr[...]


def _ssd_scan(act, dt, a, s0, tables):
    m = act.shape[0]
    n_seq = s0.shape[0]
    steps = tables.shape[1]
    c = SSD_CHUNK
    i = np.arange(c)
    tril = jnp.asarray(i[None, :] <= i[:, None], BF16)
    triu = jnp.asarray(i[None, :] >= i[:, None], BF16)
    nxb = SSD_INNER // SSD_BC

    def dir_specs(row):
        return [pl.BlockSpec((c, SSD_INNER), lambda s, tab: (tab[row, s], 0)),
                pl.BlockSpec((c, SSD_BC), lambda s, tab: (tab[row, s], nxb)),
                pl.BlockSpec((c, SSD_BC), lambda s, tab: (tab[row, s], nxb + 1)),
                pl.BlockSpec((c, DT_PAD), lambda s, tab: (tab[row, s], 0)),
                pl.BlockSpec((c, DT_PAD), lambda s, tab: (tab[row, s], 0))]

    st_spec = lambda: pl.BlockSpec((1, 2, SSD_HEADS // 2, LANES, SSD_STATE), lambda s, tab: (tab[4, s], 0, 0, 0, 0))
    in_specs = dir_specs(0) + dir_specs(1) + [pl.BlockSpec((c, c), lambda s, tab: (0, 0)),
                                              pl.BlockSpec((c, c), lambda s, tab: (0, 0)), st_spec()]
    out_specs = [pl.BlockSpec((c, SSD_INNER), lambda s, tab: (tab[0, s], 0)),
                 pl.BlockSpec((c, SSD_INNER), lambda s, tab: (tab[1, s], 0)), st_spec()]
    sd = jax.ShapeDtypeStruct
    out_shape = [sd((m, SSD_INNER), F32), sd((m, SSD_INNER), F32), sd(s0.shape, F32)]
    grid_spec = pltpu.PrefetchScalarGridSpec(
        num_scalar_prefetch=1, grid=(steps,), in_specs=in_specs, out_specs=out_specs,
        scratch_shapes=[pltpu.VMEM(s0.shape[1:], F32)])
    args = [act, act, act, dt, a] * 2 + [tril, triu, s0]
    return pl.pallas_call(
        _ssd_scan_kernel, grid_spec=grid_spec, out_shape=out_shape,
        compiler_params=_cparams(("arbitrary",)), name="ssd_scan",
    )(tables, *args)


def _ssd_post_kernel(yf_ref, yb_ref, x_ref, z_ref, d_ref, g_ref, o_ref):
    y = (x_ref[...].astype(F32) * d_ref[...] + yf_ref[...] + yb_ref[...]) * _silu(z_ref[...])
    gw = SSD_INNER // SSD_GROUPS
    for g in range(SSD_GROUPS):
        sl = slice(g * gw, (g + 1) * gw)
        yg = y[:, sl]
        yg = yg * lax.rsqrt(jnp.mean(yg * yg, axis=-1, keepdims=True) + NORM_EPS)
        o_ref[:, sl] = (yg * g_ref[:, sl]).astype(o_ref.dtype)


def _ssd_post(y_f, y_b, act, p, d_lanes, norm_g, tm):
    m = y_f.shape[0]
    w = SSD_INNER
    tok = lambda: pl.BlockSpec((tm, w), lambda i: (i, 0))
    row = lambda: pl.BlockSpec((1, w), lambda i: (0, 0))
    return pl.pallas_call(
        _ssd_post_kernel, grid=(m // tm,),
        in_specs=[tok(), tok(), tok(), pl.BlockSpec((tm, w), lambda i: (i, COL_Z // w)), row(), row()],
        out_specs=tok(), out_shape=jax.ShapeDtypeStruct((m, w), BF16),
        compiler_params=_cparams(("arbitrary",)), name="ssd_post",
    )(y_f, y_b, act, p, d_lanes, norm_g)


def ssd_mix(p, conv_w, conv_b, dt_bias, a_log, d_skip, norm_g, s0, tables, *, tm, n_ctx_tokens, ctx_len, lat_len):
    a_neg = -jnp.exp(a_log)
    act, dt, a = _ssd_prep(p, conv_w, conv_b, dt_bias, a_neg, tm=tm, n_ctx_tokens=n_ctx_tokens,
                           ctx_len=ctx_len, lat_len=lat_len)
    sh = s0.shape
    s0p = s0.reshape(sh[0], 2, SSD_HEADS // 2, LANES, SSD_STATE)
    y_f, y_b, s_fin = _ssd_scan(act, dt, a, s0p, tables)
    d_lanes = jnp.repeat(d_skip, HEAD_DIM)[None]
    y = _ssd_post(y_f, y_b, act, p, d_lanes, norm_g[None], tm)
    return y, s_fin.reshape(sh)


ATTN_SCALE = HEAD_DIM ** -0.5
MASK_BIAS = -1e30
NA_Q_ROWS = 4
NA_K_ROWS = 12


def _softmax_pv(q_heads, key_sets, lane_head0):
    out = None
    for h, qm in enumerate(q_heads):
        scores = []
        for kb, _, bias in key_sets:
            s = _dot_nt(qm, kb)
            scores.append(s if bias is None else s + bias[h])
        mx = scores[0].max(axis=-1, keepdims=True)
        for s in scores[1:]:
            mx = jnp.maximum(mx, s.max(axis=-1, keepdims=True))
        den = 0.0
        acc = 0.0
        for s, (_, vb, _) in zip(scores, key_sets):
            e = jnp.exp(s - mx)
            den = den + e.sum(axis=-1, keepdims=True)
            acc = acc + _dot(e.astype(BF16), vb)
        o = acc / den
        keep = lane_head0 if h == 0 else jnp.logical_not(lane_head0)
        o = jnp.where(keep, o, 0.0)
        out = o if out is None else out + o
    return out


def _head_queries(q):
    lane0 = lax.broadcasted_iota(jnp.int32, (1, LANES), 1) < HEAD_DIM
    qs = q * ATTN_SCALE
    return [jnp.where(lane0, qs, 0.0).astype(BF16), jnp.where(lane0, 0.0, qs).astype(BF16)], lane0


def _ctx_attn_kernel(q_ref, k_ref, v_ref, o_ref):
    q_heads, lane0 = _head_queries(q_ref[...])
    sets = [(k_ref[...].astype(BF16), v_ref[...].astype(BF16), None)]
    o_ref[...] = _softmax_pv(q_heads, sets, lane0).astype(o_ref.dtype)


def _context_attention(p, n_seq, seq):
    npair = NA_INNER // LANES
    spec = lambda col: pl.BlockSpec((seq, LANES), lambda b, g: (b, col // LANES + g))
    return pl.pallas_call(
        _ctx_attn_kernel, grid=(n_seq, npair),
        in_specs=[spec(COL_Q), spec(COL_K), spec(COL_V)],
        out_specs=pl.BlockSpec((seq, LANES), lambda b, g: (b, g)),
        out_shape=jax.ShapeDtypeStruct((n_seq * seq, NA_INNER), BF16),
        compiler_params=_cparams(("arbitrary", "arbitrary")), name="context_attention",
    )(p, p, p)


def _na_geometry(rows):
    wr = min(WIN_R, rows)
    assert wr == WIN_R and rows % NA_Q_ROWS == 0 and rows >= NA_K_ROWS
    n_rb = rows // NA_Q_ROWS
    r_ids = np.arange(rows)
    key_r0 = np.clip(r_ids - wr // 2, 0, rows - wr)
    starts = np.clip(np.arange(n_rb) * NA_Q_ROWS - wr // 2, 0, rows - NA_K_ROWS)
    sigs, types = [], []
    for rb in range(n_rb):
        rr = np.arange(rb * NA_Q_ROWS, (rb + 1) * NA_Q_ROWS)
        assert starts[rb] <= key_r0[rr].min() and key_r0[rr].max() + wr <= starts[rb] + NA_K_ROWS
        sig = (int(starts[rb] - rb * NA_Q_ROWS),) + tuple(int(x) for x in key_r0[rr] - rb * NA_Q_ROWS)
        if sig not in sigs:
            sigs.append(sig)
        types.append(sigs.index(sig))
    return n_rb, starts.astype(np.int32), np.array(types, np.int32), sigs


def _na_bias_table(rpb, rows):
    _, _, _, sigs = _na_geometry(rows)
    c_ids = np.arange(GRID_W)
    key_c0 = np.clip(c_ids - WIN_C // 2, 0, GRID_W - WIN_C)
    tabs = []
    for sig in sigs:
        start_rel, kr0_rel = sig[0], np.array(sig[1:])
        qr = np.arange(NA_Q_ROWS)[:, None, None, None]
        qc = c_ids[None, :, None, None]
        kr = (start_rel + np.arange(NA_K_ROWS))[None, None, :, None]
        kc = c_ids[None, None, None, :]
        valid = ((kr >= kr0_rel[:, None, None, None]) & (kr < kr0_rel[:, None, None, None] + WIN_R)
                 & (kc >= key_c0[None, :, None, None]) & (kc < key_c0[None, :, None, None] + WIN_C))
        off_r = np.clip(kr - qr + WIN_R - 1, 0, 2 * WIN_R - 2)
        off_c = np.clip(kc - qc + WIN_C - 1, 0, 2 * WIN_C - 2)
        off_r, off_c, valid = np.broadcast_arrays(off_r, off_c, valid)
        n_q, n_k = NA_Q_ROWS * GRID_W, NA_K_ROWS * GRID_W
        b = rpb[:, off_r.reshape(n_q, n_k), off_c.reshape(n_q, n_k)]
        tabs.append(jnp.where(jnp.asarray(valid.reshape(n_q, n_k)), b, MASK_BIAS))
    return jnp.stack(tabs, axis=1)


def _nbr_attn_kernel(tab_ref, q_ref, k_ref, v_ref, kc_ref, vc_ref, bias_ref, o_ref):
    rb = pl.program_id(2)
    start = pl.multiple_of(tab_ref[0, rb] * GRID_W, GRID_W)
    n_k = NA_K_ROWS * GRID_W
    q_heads, lane0 = _head_queries(q_ref[...])
    sets = [(k_ref[pl.ds(start, n_k), :].astype(BF16), v_ref[pl.ds(start, n_k), :].astype(BF16), bias_ref),
            (kc_ref[0].astype(BF16), vc_ref[0].astype(BF16), None)]
    o_ref[...] = _softmax_pv(q_heads, sets, lane0).astype(o_ref.dtype)


def _neighbourhood_attention(p, k_ctx, v_ctx, bias_tab, row_off_blocks, n_b, t):
    rows = t // GRID_W
    n_rb, starts, types, _ = _na_geometry(rows)
    tq = NA_Q_ROWS * GRID_W
    npair = NA_INNER // LANES
    past = k_ctx.shape[1]
    tabs = jnp.asarray(np.stack([starts, types]))
    qspec = pl.BlockSpec((tq, LANES), lambda b, g, r, tab: ((row_off_blocks + b) * n_rb + r, COL_Q // LANES + g))
    kv = lambda col: pl.BlockSpec((t, LANES), lambda b, g, r, tab: (row_off_blocks + b, col // LANES + g))
    cspec = lambda: pl.BlockSpec((1, past, LANES), lambda b, g, r, tab: (b, 0, g))
    bspec = pl.BlockSpec((2, None, tq, NA_K_ROWS * GRID_W), lambda b, g, r, tab: (g, tab[1, r], 0, 0))
    grid_spec = pltpu.PrefetchScalarGridSpec(
        num_scalar_prefetch=1, grid=(n_b, npair, n_rb),
        in_specs=[qspec, kv(COL_K), kv(COL_V), cspec(), cspec(), bspec],
        out_specs=pl.BlockSpec((tq, LANES), lambda b, g, r, tab: (b * n_rb + r, g)))
    return pl.pallas_call(
        _nbr_attn_kernel, grid_spec=grid_spec,
        out_shape=jax.ShapeDtypeStruct((n_b * t, NA_INNER), BF16),
        compiler_params=_cparams(("arbitrary", "arbitrary", "arbitrary")), name="neighbourhood_attention",
    )(tabs, p, p, p, k_ctx, v_ctx, bias_tab)


def _branch_merge_kernel(y0_ref, y1_ref, y2_ref, w_ref, g0_ref, g1_ref, g2_ref, o_ref):
    acc = _sigmoid(g0_ref[...]) * _dot(y0_ref[...], w_ref[0])
    acc += _sigmoid(g1_ref[...]) * _dot(y1_ref[...], w_ref[1])
    acc += _sigmoid(g2_ref[...]) * _dot(y2_ref[...], w_ref[2])
    o_ref[...] = acc.astype(o_ref.dtype)


def _branch_merge(ys, w_branch, p, tm, tn):
    m, kdim = ys[0].shape
    n = w_branch.shape[2]
    gb = COL_GATE // tn
    yspec = lambda: pl.BlockSpec((tm, kdim), lambda i, j: (i, 0))
    gspec = lambda b: pl.BlockSpec((tm, tn), lambda i, j: (i, gb + b * (n // tn) + j))
    return pl.pallas_call(
        _branch_merge_kernel, grid=(m // tm, n // tn),
        in_specs=[yspec(), yspec(), yspec(), pl.BlockSpec((N_BRANCHES, kdim, tn), lambda i, j: (0, 0, j)),
                  gspec(0), gspec(1), gspec(2)],
        out_specs=pl.BlockSpec((tm, tn), lambda i, j: (i, j)),
        out_shape=jax.ShapeDtypeStruct((m, n), BF16),
        compiler_params=_cparams(("arbitrary", "arbitrary")), name="branch_merge",
    )(ys[0], ys[1], ys[2], w_branch, p, p, p)


def _proj_residual_kernel(a_ref, w_ref, x_ref, mod_ref, o_ref, *, gate_row):
    g = mod_ref[0][gate_row:gate_row + 1, :]
    o_ref[...] = x_ref[...] + g * _dot(a_ref[...], w_ref[...])


def _proj_residual(a, w, x, mod, seg_tokens, gate_row, tm, tn):
    m, kdim = a.shape
    n = w.shape[1]
    return pl.pallas_call(
        functools.partial(_proj_residual_kernel, gate_row=gate_row), grid=(m // tm, n // tn),
        in_specs=[pl.BlockSpec((tm, kdim), lambda i, j: (i, 0)),
                  pl.BlockSpec((kdim, tn), lambda i, j: (0, j)),
                  pl.BlockSpec((tm, tn), lambda i, j: (i, j)),
                  pl.BlockSpec((1, SUBLANES, tn), lambda i, j: ((i * tm) // seg_tokens, 0, j))],
        out_specs=pl.BlockSpec((tm, tn), lambda i, j: (i, j)),
        out_shape=jax.ShapeDtypeStruct((m, n), F32),
        compiler_params=_cparams(("arbitrary", "arbitrary")), name="proj_residual",
    )(a, w, x, mod)


def _final_norm_kernel(x_ref, g_ref, o_ref):
    x = x_ref[...]
    o_ref[...] = x * lax.rsqrt(jnp.mean(x * x, axis=-1, keepdims=True) + NORM_EPS) * g_ref[...]


def _final_norm(x, g, tm):
    m, d = x.shape
    return pl.pallas_call(
        _final_norm_kernel, grid=(m // tm,),
        in_specs=[pl.BlockSpec((tm, d), lambda i: (i, 0)), pl.BlockSpec((1, d), lambda i: (0, 0))],
        out_specs=pl.BlockSpec((tm, d), lambda i: (i, 0)),
        out_shape=jax.ShapeDtypeStruct((m, d), F32),
        compiler_params=_cparams(("arbitrary",)), name="final_norm",
    )(x, g)


MOE_TM = 256
MOE_TN = 1024
NEG_BIG = -3.0e38


def _moe_route_kernel(x_ref, g_ref, mod_ref, wh_ref, wl_ref, b_ref, tril_ref,
                      hn_ref, idx_ref, rank_ref, prob_ref, cnt_ref, carry, *, shift_row, scale_row):
    @pl.when(pl.program_id(0) == 0)
    def _():
        carry[...] = jnp.zeros_like(carry)

    x = x_ref[...]
    y = x * lax.rsqrt(jnp.mean(x * x, axis=-1, keepdims=True) + NORM_EPS) * g_ref[...]
    m = mod_ref[0]
    hn = y * (1.0 + m[scale_row:scale_row + 1, :]) + m[shift_row:shift_row + 1, :]
    hn_ref[...] = hn
    hi = hn.astype(BF16)
    lo = (hn - hi.astype(F32)).astype(BF16)
    wh = wh_ref[...]
    logits = _dot(hi, wh) + _dot(lo, wh) + _dot(hi, wl_ref[...]) + b_ref[...]

    lane_i = lax.broadcasted_iota(jnp.int32, logits.shape, 1)
    lane = lane_i.astype(F32)
    work = logits
    sel = jnp.zeros(logits.shape, jnp.bool_)
    picks = []
    m0 = None
    for k in range(TOP_K):
        mx = work.max(axis=-1, keepdims=True)
        idx = jnp.min(jnp.where(work == mx, lane, float(LANES)), axis=-1, keepdims=True)
        onehot = lane == idx
        sel = jnp.logical_or(sel, onehot)
        work = jnp.where(onehot, NEG_BIG, work)
        picks.append((idx, onehot))
        if k == 0:
            m0 = mx
    e = jnp.where(sel, jnp.exp(logits - m0), 0.0)
    probs = e / e.sum(axis=-1, keepdims=True)

    self_f = jnp.where(sel, 1.0, 0.0)
    before = _dot(tril_ref[...], self_f.astype(BF16)) + carry[...]
    carry[...] = before[-1:, :] + self_f[-1:, :]
    cnt_ref[...] = jnp.broadcast_to(carry[...], cnt_ref.shape)

    idx_o = jnp.zeros(logits.shape, jnp.int32)
    rank_o = jnp.zeros(logits.shape, F32)
    prob_o = jnp.zeros(logits.shape, F32)
    for k, (idx, onehot) in enumerate(picks):
        at_k = lane_i == k
        idx_o = jnp.where(at_k, idx.astype(jnp.int32), idx_o)
        rank_o = jnp.where(at_k, jnp.where(onehot, before, 0.0).sum(axis=-1, keepdims=True), rank_o)
        prob_o = jnp.where(at_k, jnp.where(onehot, probs, 0.0).sum(axis=-1, keepdims=True), prob_o)
    idx_ref[...] = idx_o
    rank_ref[...] = rank_o
    prob_ref[...] = prob_o


def _moe_route(x, g, mod, router_w, router_b, seg_tokens, shift_row, scale_row, tm):
    m, d = x.shape
    ne = router_w.shape[1]
    w = jnp.pad(router_w, ((0, 0), (0, LANES - ne)))
    wh = w.astype(BF16)
    wl = (w - wh.astype(F32)).astype(BF16)
    b = jnp.concatenate([router_b, jnp.full((LANES - ne,), MASK_BIAS, F32)])[None]
    i = np.arange(tm)
    tril = jnp.asarray(i[None, :] < i[:, None], BF16)
    row = lambda: pl.BlockSpec((tm, LANES), lambda t: (t, 0))
    sd = jax.ShapeDtypeStruct
    kern = functools.partial(_moe_route_kernel, shift_row=shift_row, scale_row=scale_row)
    return pl.pallas_call(
        kern, grid=(m // tm,),
        in_specs=[pl.BlockSpec((tm, d), lambda t: (t, 0)), pl.BlockSpec((1, d), lambda t: (0, 0)),
                  pl.BlockSpec((1, SUBLANES, d), lambda t: ((t * tm) // seg_tokens, 0, 0)),
                  pl.BlockSpec((d, LANES), lambda t: (0, 0)), pl.BlockSpec((d, LANES), lambda t: (0, 0)),
                  pl.BlockSpec((1, LANES), lambda t: (0, 0)), pl.BlockSpec((tm, tm), lambda t: (0, 0))],
        out_specs=[pl.BlockSpec((tm, d), lambda t: (t, 0)), row(), row(), row(),
                   pl.BlockSpec((SUBLANES, LANES), lambda t: (0, 0))],
        out_shape=[sd((m, d), F32), sd((m, LANES), jnp.int32), sd((m, LANES), F32), sd((m, LANES), F32),
                   sd((SUBLANES, LANES), F32)],
        scratch_shapes=[pltpu.VMEM((1, LANES), F32)],
        compiler_params=_cparams(("arbitrary",)), name="moe_route",
    )(x, g, mod, wh, wl, b, tril)


def _row_copy(src, dst, s_row, d_row, sem):
    return pltpu.make_async_copy(src.at[pl.ds(s_row, 1)], dst.at[pl.ds(d_row, 1)], sem)


def _moe_dispatch_kernel(slot_ref, hn_ref, zeros_ref, xs_ref, sem, *, tm):
    del zeros_ref
    base = pl.program_id(0) * tm

    def issue(t, c):
        for k in range(TOP_K):
            _row_copy(hn_ref, xs_ref, base + t, slot_ref[t * TOP_K + k], sem).start()
        return c
    lax.fori_loop(0, tm, issue, 0)

    def drain(t, c):
        for k in range(TOP_K):
            _row_copy(hn_ref, xs_ref, base + t, slot_ref[t * TOP_K + k], sem).wait()
        return c
    lax.fori_loop(0, tm, drain, 0)


def _moe_dispatch(hn, slot_flat, cap, tm):
    m, d = hn.shape
    return pl.pallas_call(
        functools.partial(_moe_dispatch_kernel, tm=tm), grid=(m // tm,),
        in_specs=[pl.BlockSpec((tm * TOP_K,), lambda t: (t,), memory_space=pltpu.SMEM),
                  pl.BlockSpec(memory_space=pl.ANY), pl.BlockSpec(memory_space=pl.ANY)],
        out_specs=pl.BlockSpec(memory_space=pl.ANY),
        out_shape=jax.ShapeDtypeStruct((cap, d), F32),
        scratch_shapes=[pltpu.SemaphoreType.DMA(())],
        input_output_aliases={2: 0},
        compiler_params=_cparams(("arbitrary",)), name="moe_dispatch",
    )(slot_flat, hn, jnp.zeros((cap, d), F32))


def _swiglu(gate, up):
    gate = jnp.minimum(gate, SWIGLU_LIMIT)
    up = jnp.clip(up, -SWIGLU_LIMIT, SWIGLU_LIMIT)
    return gate * _sigmoid(gate * SWIGLU_ALPHA) * (up + 1.0)


def _moe_gu_kernel(meta_ref, xs_ref, wg_ref, wu_ref, bg_ref, bu_ref, h_ref, wg_s, wu_s):
    i = pl.program_id(1)

    @pl.when(meta_ref[1, i] == 1)
    def _():
        wg_s[...] = wg_ref[...].astype(BF16)
        wu_s[...] = wu_ref[...].astype(BF16)

    @pl.when(meta_ref[2, i] == 1)
    def _():
        xb = xs_ref[...].astype(BF16)
        gate = _dot(xb, wg_s[...]) + bg_ref[...]
        up = _dot(xb, wu_s[...]) + bu_ref[...]
        h_ref[...] = _swiglu(gate, up).astype(h_ref.dtype)

    @pl.when(meta_ref[2, i] == 0)
    def _():
        h_ref[...] = jnp.zeros_like(h_ref)


def _moe_gu(xs, meta, w_gu, b_gu, layer, tm, tn):
    cap, d = xs.shape
    f = w_gu.shape[3] // 2
    nj = f // tn
    wspec = lambda off: pl.BlockSpec((None, None, d, tn), lambda j, i, mt: (layer, mt[0, i], 0, off + j))
    bspec = lambda off: pl.BlockSpec((None, None, 1, tn), lambda j, i, mt: (layer, mt[0, i], 0, off + j))
    grid_spec = pltpu.PrefetchScalarGridSpec(
        num_scalar_prefetch=1, grid=(nj, cap // tm),
        in_specs=[pl.BlockSpec((tm, d), lambda j, i, mt: (i, 0)), wspec(0), wspec(nj), bspec(0), bspec(nj)],
        out_specs=pl.BlockSpec((tm, tn), lambda j, i, mt: (i, j)),
        scratch_shapes=[pltpu.VMEM((d, tn), BF16), pltpu.VMEM((d, tn), BF16)])
    b4 = b_gu.reshape(b_gu.shape[0], b_gu.shape[1], 1, b_gu.shape[2])
    return pl.pallas_call(
        _moe_gu_kernel, grid_spec=grid_spec, out_shape=jax.ShapeDtypeStruct((cap, f), BF16),
        compiler_params=_cparams(("arbitrary", "arbitrary")), name="moe_gate_up",
    )(meta, xs, w_gu, w_gu, b4, b4)


def _moe_down_kernel(meta_ref, h_ref, w_ref, b_ref, y_ref, w_s):
    i = pl.program_id(1)

    @pl.when(meta_ref[1, i] == 1)
    def _():
        w_s[...] = w_ref[...].astype(BF16)

    @pl.when(meta_ref[2, i] == 1)
    def _():
        y_ref[...] = _dot(h_ref[...], w_s[...]) + b_ref[...]

    @pl.when(meta_ref[2, i] == 0)
    def _():
        y_ref[...] = jnp.zeros_like(y_ref)


def _moe_down(h, meta, w_down, b_down, layer, tm, tn):
    cap, f = h.shape
    d = w_down.shape[3]
    grid_spec = pltpu.PrefetchScalarGridSpec(
        num_scalar_prefetch=1, grid=(d // tn, cap // tm),
        in_specs=[pl.BlockSpec((tm, f), lambda j, i, mt: (i, 0)),
                  pl.BlockSpec((None, None, f, tn), lambda j, i, mt: (layer, mt[0, i], 0, j)),
                  pl.BlockSpec((None, None, 1, tn), lambda j, i, mt: (layer, mt[0, i], 0, j))],
        out_specs=pl.BlockSpec((tm, tn), lambda j, i, mt: (i, j)),
        scratch_shapes=[pltpu.VMEM((f, tn), BF16)])
    b4 = b_down.reshape(b_down.shape[0], b_down.shape[1], 1, b_down.shape[2])
    return pl.pallas_call(
        _moe_down_kernel, grid_spec=grid_spec, out_shape=jax.ShapeDtypeStruct((cap, d), F32),
        compiler_params=_cparams(("arbitrary", "arbitrary")), name="moe_down",
    )(meta, h, w_down, b4)


def _moe_combine_kernel(slot_ref, yb_ref, x_ref, prob_ref, mod_ref, o_ref, buf, sem, *, tm, gate_row):
    def issue(t, c):
        for k in range(TOP_K):
            _row_copy(yb_ref, buf.at[k], slot_ref[t * TOP_K + k], t, sem).start()
        return c
    lax.fori_loop(0, tm, issue, 0)

    def drain(t, c):
        for k in range(TOP_K):
            _row_copy(yb_ref, buf.at[k], slot_ref[t * TOP_K + k], t, sem).wait()
        return c
    lax.fori_loop(0, tm, drain, 0)

    pr = prob_ref[...]
    acc = pr[:, 0:1] * buf[0]
    for k in range(1, TOP_K):
        acc += pr[:, k:k + 1] * buf[k]
    g = mod_ref[0][gate_row:gate_row + 1, :]
    o_ref[...] = x_ref[...] + g * acc


def _moe_combine(yb, slot_flat, prob, x, mod, seg_tokens, gate_row, tm):
    m, d = x.shape
    kern = functools.partial(_moe_combine_kernel, tm=tm, gate_row=gate_row)
    return pl.pallas_call(
        kern, grid=(m // tm,),
        in_specs=[pl.BlockSpec((tm * TOP_K,), lambda t: (t,), memory_space=pltpu.SMEM),
                  pl.BlockSpec(memory_space=pl.ANY),
                  pl.BlockSpec((tm, d), lambda t: (t, 0)),
                  pl.BlockSpec((tm, LANES), lambda t: (t, 0)),
                  pl.BlockSpec((1, SUBLANES, d), lambda t: ((t * tm) // seg_tokens, 0, 0))],
        out_specs=pl.BlockSpec((tm, d), lambda t: (t, 0)),
        out_shape=jax.ShapeDtypeStruct((m, d), F32),
        scratch_shapes=[pltpu.VMEM((TOP_K, tm, d), F32), pltpu.SemaphoreType.DMA(())],
        compiler_params=_cparams(("arbitrary",)), name="moe_combine",
    )(slot_flat, yb, x, prob, mod)


def _moe_layer(x, g, mod, router_w, router_b, w_gu, b_gu, w_down, b_down, layer, seg_tokens):
    m, d = x.shape
    hn, idx, rank, prob, cnt = _moe_route(x, g, mod, router_w, router_b, seg_tokens, 3, 4, MOE_TM)
    counts = cnt[0, :N_EXPERTS].astype(jnp.int32)
    padded = (counts + MOE_TM - 1) // MOE_TM * MOE_TM
    pad_end = jnp.cumsum(padded)
    starts = pad_end - padded
    e4 = idx[:, :TOP_K]
    slot = starts[e4] + rank[:, :TOP_K].astype(jnp.int32)
    slot_flat = slot.reshape(-1)
    n_blocks = (m * TOP_K) // MOE_TM + N_EXPERTS
    cap = n_blocks * MOE_TM
    blk0 = jnp.arange(n_blocks, dtype=jnp.int32) * MOE_TM
    block_e = jnp.minimum(jnp.searchsorted(pad_end, blk0, side='right'), N_EXPERTS - 1).astype(jnp.int32)
    is_new = jnp.concatenate([jnp.ones((1,), jnp.int32), (block_e[1:] != block_e[:-1]).astype(jnp.int32)])
    used = (blk0 < pad_end[-1]).astype(jnp.int32)
    meta = jnp.stack([block_e, is_new, used])
    xs = _moe_dispatch(hn, slot_flat, cap, MOE_TM)
    h = _moe_gu(xs, meta, w_gu, b_gu, layer, MOE_TM, MOE_TN)
    yb = _moe_down(h, meta, w_down, b_down, layer, MOE_TM, MOE_TN)
    return _moe_combine(yb, slot_flat, prob, x, mod, seg_tokens, 5, MOE_TM)


def _split_last(x, sizes):
    cuts = [int(s) for s in np.cumsum(sizes)[:-1]]
    return jnp.split(x, cuts, axis=-1)


def _relayout_w_in(w):
    z, xbc, dt, rw, q, k, v, gate = _split_last(
        w, (SSD_INNER, SSD_XBC, 2 * SSD_HEADS, RWKV_COLS, NA_INNER, NA_INNER, NA_INNER, N_BRANCHES * D_MODEL))
    padc = lambda a, n: jnp.pad(a, ((0, 0), (0, n - a.shape[1])))
    parts = [z, padc(rw, RWKV_PAD), xbc, padc(dt, DT_PAD), q, k, padc(v, COL_GATE - COL_V), gate]
    out = jnp.concatenate(parts, axis=1)
    return padc(out, IN_COLS_PAD).astype(BF16)


def kernel(x_prompt, x_sample, cache_na_k, cache_na_v, state_ssd, state_rwkv, c, c_ctx, norm1_g, norm2_g, w_mod, b_mod, w_in, ssd_conv_w, ssd_conv_b, ssd_dt_bias, ssd_a_log, ssd_d, ssd_norm_g, rwkv_mu, rwkv_w0, rwkv_w2, rwkv_a0, rwkv_a2, rwkv_g2, rwkv_kk, rwkv_ka, rwkv_rk, rwkv_ln_g, rwkv_ln_b, na_rpb, w_branch, w_out, router_w, router_b, moe_w_gu, moe_b_gu, moe_w_down, moe_b_down, final_g):
    nb, seq, d = x_prompt.shape
    db, dseq, _ = x_sample.shape
    n_ctx = nb * seq
    n_lat = db * dseq
    assert n_ctx == dseq, "modulation segments assume context tokens fill exactly one latent-length segment"
    seg_tokens = dseq
    n_seg = 1 + db
    m = n_ctx + n_lat
    rwkv_tm = 256
    tables = _scan_tables(nb, seq // RWKV_CHUNK, db, dseq // RWKV_CHUNK)
    ssd_tables = _scan_tables(nb, seq // SSD_CHUNK, db, dseq // SSD_CHUNK)

    x = jnp.concatenate([x_prompt.reshape(n_ctx, d), x_sample.reshape(n_lat, d)], axis=0)
    cond = jnp.zeros((SUBLANES, d), F32).at[0].set(c_ctx).at[1:1 + db].set(c)

    ctx_k, ctx_v, ctx_ssd, ctx_rwkv = [], [], [], []
    for l in range(DEPTH):
        mod = _modulation(cond, w_mod[l], b_mod[l])
        mod = mod.reshape(SUBLANES, 6, d)[:n_seg]
        mod = jnp.pad(mod, ((0, 0), (0, SUBLANES - 6), (0, 0)))
        p = _norm_matmul(x, norm1_g[l][None], mod, _relayout_w_in(w_in[l]), seg_tokens, 0, 1, 1024, IN_TN)

        lw = _rwkv_layer_weights(rwkv_mu[l], rwkv_w0[l], rwkv_w2[l], rwkv_a0[l], rwkv_a2[l], rwkv_g2[l],
                                 rwkv_kk[l], rwkv_ka[l], rwkv_rk[l], rwkv_tm)
        lw['ln_g'] = rwkv_ln_g[l][None]
        lw['ln_b'] = rwkv_ln_b[l][None]
        s0 = jnp.concatenate([jnp.zeros((nb,) + state_rwkv.shape[2:], F32), state_rwkv[:, l]], axis=0)
        y_rwkv, s_rwkv = rwkv_mix(p, lw, _pair_blockdiag(s0), tables, tm=rwkv_tm, n_ctx_tokens=n_ctx,
                                  ctx_len=seq, lat_len=dseq, pairs=4)
        ctx_rwkv.append(_pair_unblock(s_rwkv[:nb]))

        s0 = jnp.concatenate([jnp.zeros((nb,) + state_ssd.shape[2:], F32), state_ssd[:, l]], axis=0)
        y_ssd, s_ssd = ssd_mix(p, ssd_conv_w[l], ssd_conv_b[l], ssd_dt_bias[l], ssd_a_log[l], ssd_d[l],
                               ssd_norm_g[l], s0, ssd_tables, tm=256, n_ctx_tokens=n_ctx, ctx_len=seq, lat_len=dseq)
        ctx_ssd.append(s_ssd[:nb])

        hd = (NA_HEADS, HEAD_DIM)
        ctx_k.append(p[:n_ctx, COL_K:COL_K + NA_INNER].reshape((nb, seq) + hd))
        ctx_v.append(p[:n_ctx, COL_V:COL_V + NA_INNER].reshape((nb, seq) + hd))
        y_na_c = _context_attention(p, nb, seq)
        past = cache_na_k.shape[2]
        y_na_l = _neighbourhood_attention(p, cache_na_k[:, l].reshape(db, past, NA_INNER),
                                          cache_na_v[:, l].reshape(db, past, NA_INNER),
                                          _na_bias_table(na_rpb[l], dseq // GRID_W), n_ctx // dseq, db, dseq)
        y_na = jnp.concatenate([y_na_c, y_na_l], axis=0)

        merged = _branch_merge((y_ssd, y_rwkv, y_na), w_branch[l].astype(BF16), p, 512, 512)
        x = _proj_residual(merged, w_out[l].astype(BF16), x, mod, seg_tokens, 2, 1024, 512)

        x = _moe_layer(x, norm2_g[l][None], mod, router_w[l], router_b[l], moe_w_gu, moe_b_gu,
                       moe_w_down, moe_b_down, l, seg_tokens)

    y = _final_norm(x, final_g[None], 1024)
    y_prompt = y[:n_ctx].reshape(nb, seq, d)
    y_sample = y[n_ctx:].reshape(db, dseq, d)
    return (y_prompt, y_sample, jnp.stack(ctx_k, axis=1), jnp.stack(ctx_v, axis=1),
            jnp.stack(ctx_ssd, axis=1), jnp.stack(ctx_rwkv, axis=1))
```

```python
import functools
import math

import jax
import jax.numpy as jnp
import numpy as np
from jax import lax
from jax.experimental import pallas as pl
from jax.experimental.pallas import tpu as pltpu

F32 = jnp.float32
BF16 = jnp.bfloat16

D_MODEL = 2048
DEPTH = 4
GRID_W = 64
HEAD_DIM = 64
MIX_WIDTH = D_MODEL // 2
N_BRANCHES = 3
SSD_HEADS = MIX_WIDTH // HEAD_DIM
SSD_INNER = MIX_WIDTH
SSD_GROUPS = 2
SSD_STATE = 128
SSD_CHUNK = 128
SSD_CONV = 3
SSD_XBC = SSD_INNER + 2 * SSD_GROUPS * SSD_STATE
RWKV_HEADS = MIX_WIDTH // HEAD_DIM
RWKV_INNER = MIX_WIDTH
RWKV_DECAY_RANK = 64
RWKV_ICLR_RANK = 64
RWKV_GATE_RANK = 160
RWKV_COLS = 3 * RWKV_INNER + 2 * RWKV_DECAY_RANK + 2 * RWKV_ICLR_RANK + RWKV_GATE_RANK
RWKV_LN_EPS = 64e-5
NA_HEADS = MIX_WIDTH // HEAD_DIM
NA_INNER = MIX_WIDTH
WIN_R = 8
WIN_C = 16
N_EXPERTS = 32
TOP_K = 4
D_FF = D_MODEL
SWIGLU_LIMIT = 7.0
SWIGLU_ALPHA = 1.702
NORM_EPS = 1e-6

LANES = 128
SUBLANES = 8
VMEM_LIMIT = 56 * 1024 * 1024

RWKV_PAD = 3584
DT_PAD = LANES
COL_Z = 0
COL_RWKV = COL_Z + SSD_INNER
COL_XBC = COL_RWKV + RWKV_PAD
COL_DT = COL_XBC + SSD_XBC
COL_Q = COL_DT + DT_PAD
COL_K = COL_Q + NA_INNER
COL_V = COL_K + NA_INNER
IN_TN = 512
COL_QKV_END = COL_V + NA_INNER
COL_GATE = -(-COL_QKV_END // IN_TN) * IN_TN
IN_COLS_USED = COL_GATE + N_BRANCHES * D_MODEL
IN_COLS_PAD = -(-IN_COLS_USED // IN_TN) * IN_TN

RWKV_CHUNK = 64
DECAY_SCALE = math.exp(-0.5)


def _cparams(sem):
    return pltpu.CompilerParams(dimension_semantics=sem, vmem_limit_bytes=VMEM_LIMIT)


def _dot(a, b):
    return jnp.dot(a, b, preferred_element_type=F32)


def _dot_nt(a, b):
    return lax.dot_general(a, b, (((1,), (1,)), ((), ())), preferred_element_type=F32)


def _split3(x):
    hi = x.astype(BF16)
    r1 = x - hi.astype(F32)
    mid = r1.astype(BF16)
    lo = (r1 - mid.astype(F32)).astype(BF16)
    return hi, mid, lo


def _dot_const_lhs3(c_bf16, x):
    hi, mid, lo = _split3(x)
    return _dot(c_bf16, hi) + _dot(c_bf16, mid) + _dot(c_bf16, lo)


def _dot_const_rhs2(x, c_bf16):
    hi = x.astype(BF16)
    lo = (x - hi.astype(F32)).astype(BF16)
    return _dot(hi, c_bf16) + _dot(lo, c_bf16)


def _sigmoid(x):
    return 1.0 / (1.0 + jnp.exp(-x))


def _silu(x):
    return x * _sigmoid(x)


def _mod_kernel(c_ref, w_ref, b_ref, o_ref):
    a = _silu(c_ref[...]).astype(BF16)
    o_ref[...] = _dot(a, w_ref[...].astype(BF16)) + b_ref[...]


def _modulation(cond, w_mod, b_mod):
    n = w_mod.shape[1]
    tn = 1024
    return pl.pallas_call(
        _mod_kernel,
        grid=(n // tn,),
        in_specs=[pl.BlockSpec((SUBLANES, D_MODEL), lambda j: (0, 0)),
                  pl.BlockSpec((D_MODEL, tn), lambda j: (0, j)),
                  pl.BlockSpec((1, tn), lambda j: (0, j))],
        out_specs=pl.BlockSpec((SUBLANES, tn), lambda j: (0, j)),
        out_shape=jax.ShapeDtypeStruct((SUBLANES, n), F32),
        compiler_params=_cparams(("arbitrary",)),
        name="adaln_modulation",
    )(cond, w_mod, b_mod.reshape(1, n))


def _norm_matmul_kernel(x_ref, g_ref, mod_ref, w_ref, o_ref, h_ref, *, shift_row, scale_row):
    @pl.when(pl.program_id(1) == 0)
    def _():
        x = x_ref[...]
        y = x * lax.rsqrt(jnp.mean(x * x, axis=-1, keepdims=True) + NORM_EPS)
        y = y * g_ref[...]
        m = mod_ref[0]
        y = y * (1.0 + m[scale_row:scale_row + 1, :]) + m[shift_row:shift_row + 1, :]
        h_ref[...] = y.astype(BF16)

    o_ref[...] = _dot(h_ref[...], w_ref[...]).astype(o_ref.dtype)


def _norm_matmul(x, g, mod, w, seg_tokens, shift_row, scale_row, tm, tn, out_dtype=F32):
    m, d = x.shape
    n = w.shape[1]
    kern = functools.partial(_norm_matmul_kernel, shift_row=shift_row, scale_row=scale_row)
    return pl.pallas_call(
        kern,
        grid=(m // tm, n // tn),
        in_specs=[pl.BlockSpec((tm, d), lambda i, j: (i, 0)),
                  pl.BlockSpec((1, d), lambda i, j: (0, 0)),
                  pl.BlockSpec((1, SUBLANES, d), lambda i, j: ((i * tm) // seg_tokens, 0, 0)),
                  pl.BlockSpec((d, tn), lambda i, j: (0, j))],
        out_specs=pl.BlockSpec((tm, tn), lambda i, j: (i, j)),
        out_shape=jax.ShapeDtypeStruct((m, n), out_dtype),
        scratch_shapes=[pltpu.VMEM((tm, d), BF16)],
        compiler_params=_cparams(("arbitrary", "arbitrary")),
        name="norm_matmul",
    )(x, g, mod, w)


def _shifted(x, prev8, next8, mu_p, mu_n, has_prev, has_next):
    tm = x.shape[0]
    rows = lax.broadcasted_iota(jnp.int32, x.shape, 0)
    p_row = jnp.where(has_prev, prev8[SUBLANES - 1:SUBLANES, :], 0.0)
    n_row = jnp.where(has_next, next8[0:1, :], 0.0)
    x_prev = jnp.where(rows == 0, p_row, pltpu.roll(x, 1, axis=0))
    x_next = jnp.where(rows == tm - 1, n_row, pltpu.roll(x, tm - 1, axis=0))
    return x + mu_p * (x_prev - x) + mu_n * (x_next - x)


def _rwkv_prep_kernel(
        r_ref, r_p, r_n, k_ref, k_p, k_n, v_ref, v_p, v_n,
        wl_ref, wl_p, wl_n, al_ref, al_p, al_n, gl_ref, gl_p, gl_n,
        mu_r, mu_k, mu_v, mu_wl, mu_al, mu_gl,
        w0_ref, w2_ref, a0_ref, a2_ref, g2_ref, kkp_ref, kap_ref, rk_ref,
        tril_ref, triu_ref, ones_ref, seg_ref,
        alpha_o, beta_o, kappa_o, rho_o, kappa_p_o, beta_p_o, v_o, bonus_o, g_o, pc_o,
        *, tm, n_ctx_tokens, ctx_len, lat_len):
    i = pl.program_id(0)
    tok0 = i * tm
    in_ctx = tok0 < n_ctx_tokens
    pos = jnp.where(in_ctx, tok0 % ctx_len, (tok0 - n_ctx_tokens) % lat_len)
    seq_len = jnp.where(in_ctx, ctx_len, lat_len)
    has_prev = pos != 0
    has_next = pos + tm != seq_len

    def sh(ref, p, n, mu):
        return _shifted(ref[...], p[...], n[...], mu[0:1, :], mu[1:2, :], has_prev, has_next)

    r = sh(r_ref, r_p, r_n, mu_r)
    k = sh(k_ref, k_p, k_n, mu_k)
    v = sh(v_ref, v_p, v_n, mu_v)
    w_lo = sh(wl_ref, wl_p, wl_n, mu_wl)
    a_lo = sh(al_ref, al_p, al_n, mu_al)
    g_lo = sh(gl_ref, gl_p, gl_n, mu_gl)

    seg = seg_ref[...]
    kk = k * kkp_ref[...]
    ss = _dot_const_rhs2(kk * kk, seg)
    kk = kk / jnp.maximum(jnp.sqrt(ss), 1e-12)

    tw = jnp.tanh(w_lo).astype(BF16)
    ab = a_lo.astype(BF16)
    g_o[...] = _dot(_sigmoid(g_lo).astype(BF16), g2_ref[...])
    v_o[...] = v.astype(BF16)

    tri = (tril_ref[...], triu_ref[...])
    ones_blk = ones_ref[...]
    kap = kap_ref[...]
    k_sum = jnp.zeros_like(k)
    nch = tm // RWKV_CHUNK
    for d in range(2):
        logw = w0_ref[d] + _dot(tw, w2_ref[d])
        lw = -DECAY_SCALE * _sigmoid(logw)
        a = _sigmoid(a0_ref[d] + _dot(ab, a2_ref[d]))
        b = kk * a
        k_d = k * (1.0 + (a - 1.0) * kap)
        k_sum = k_sum + k_d
        cum = _dot_const_lhs3(tri[d], lw)
        tot = _dot_const_lhs3(ones_blk, lw)
        e_neg = jnp.exp(-cum)
        e_rest = jnp.exp(tot - cum)
        alpha_o[d] = (kk * jnp.exp(cum - lw)).astype(BF16)
        beta_o[d] = (b * e_neg).astype(BF16)
        kappa_o[d] = (k_d * e_neg).astype(BF16)
        rho_o[d] = (r * jnp.exp(cum)).astype(BF16)
        kappa_p_o[d] = (k_d * e_rest).astype(BF16)
        beta_p_o[d] = (b * e_rest).astype(BF16)
        pcs = jnp.exp(tot)
        for c in range(nch):
            pc_o[d, c] = pcs[c * RWKV_CHUNK:c * RWKV_CHUNK + 1, :]
    bonus_o[...] = _dot_const_rhs2(r * k_sum * rk_ref[...], seg) * v


def _rwkv_prep(p, lw, *, tm, n_ctx_tokens, ctx_len, lat_len):
    m = p.shape[0]
    nlb = RWKV_INNER // LANES
    hb = tm // SUBLANES
    last8 = m // SUBLANES - 1
    cb = COL_RWKV // LANES

    def tile(col_blk, width=LANES, per_j=True):
        wb = width // LANES
        if per_j:
            cm = lambda i, j: (i, col_blk + j)
            cp = lambda i, j: (jnp.maximum(i * hb - 1, 0), col_blk + j)
            cn = lambda i, j: (jnp.minimum((i + 1) * hb, last8), col_blk + j)
        else:
            cm = lambda i, j: (i, col_blk // wb)
            cp = lambda i, j: (jnp.maximum(i * hb - 1, 0), col_blk // wb)
            cn = lambda i, j: (jnp.minimum((i + 1) * hb, last8), col_blk // wb)
        return [pl.BlockSpec((tm, width), cm), pl.BlockSpec((SUBLANES, width), cp),
                pl.BlockSpec((SUBLANES, width), cn)]

    gw = 2 * LANES
    in_specs = (tile(cb) + tile(cb + nlb) + tile(cb + 2 * nlb)
                + tile(cb + 3 * nlb, per_j=False) + tile(cb + 3 * nlb + 1, per_j=False)
                + tile(cb + 3 * nlb + 2, width=gw, per_j=False))
    per_lane = lambda rows: pl.BlockSpec((rows, LANES), lambda i, j: (0, j))
    const2 = lambda a, b: pl.BlockSpec((a, b), lambda i, j: (0, 0))
    in_specs += [per_lane(2), per_lane(2), per_lane(2), const2(2, LANES), const2(2, LANES), const2(2, gw)]
    in_specs += [pl.BlockSpec((2, 1, LANES), lambda i, j: (0, 0, j)),
                 pl.BlockSpec((2, LANES, LANES), lambda i, j: (0, 0, j)),
                 pl.BlockSpec((2, 1, LANES), lambda i, j: (0, 0, j)),
                 pl.BlockSpec((2, LANES, LANES), lambda i, j: (0, 0, j)),
                 pl.BlockSpec((gw, LANES), lambda i, j: (0, j)),
                 per_lane(1), per_lane(1), per_lane(1),
                 const2(tm, tm), const2(tm, tm), const2(tm, tm), const2(LANES, LANES)]
    tok_d = lambda: pl.BlockSpec((2, tm, LANES), lambda i, j: (0, i, j))
    tok = lambda: pl.BlockSpec((tm, LANES), lambda i, j: (i, j))
    nch = tm // RWKV_CHUNK
    out_specs = [tok_d() for _ in range(6)] + [tok(), tok(), tok(),
                 pl.BlockSpec((2, nch, 1, LANES), lambda i, j: (0, i, 0, j))]
    sd = jax.ShapeDtypeStruct
    out_shape = [sd((2, m, RWKV_INNER), BF16) for _ in range(6)] + [
        sd((m, RWKV_INNER), BF16), sd((m, RWKV_INNER), F32), sd((m, RWKV_INNER), F32),
        sd((2, m // RWKV_CHUNK, 1, RWKV_INNER), F32)]
    kern = functools.partial(_rwkv_prep_kernel, tm=tm, n_ctx_tokens=n_ctx_tokens, ctx_len=ctx_len, lat_len=lat_len)
    args = [p] * 18 + [lw['mu_r'], lw['mu_k'], lw['mu_v'], lw['mu_wl'], lw['mu_al'], lw['mu_gl'],
                       lw['w0'], lw['w2'], lw['a0'], lw['a2'], lw['g2'], lw['kk'], lw['ka'], lw['rk'],
                       lw['tril'], lw['triu'], lw['ones'], lw['seg']]
    return pl.pallas_call(
        kern, grid=(m // tm, nlb), in_specs=in_specs, out_specs=out_specs, out_shape=out_shape,
        compiler_params=_cparams(("arbitrary", "arbitrary")), name="rwkv_prep",
    )(*args)


def _rwkv_scan_kernel(tab_ref,
                      al_f, be_f, ka_f, rh_f, kp_f, bp_f, v_f, pc_f,
                      al_b, be_b, ka_b, rh_b, kp_b, bp_b, v_b, pc_b,
                      s0_ref, of_ref, ob_ref, sfin_ref, s_scr, *, pairs):
    s = pl.program_id(1)
    is_first = tab_ref[2, s] == 1
    is_last = tab_ref[3, s] == 1

    @pl.when(is_first)
    def _():
        s_scr[...] = s0_ref[0]

    c2 = 2 * RWKV_CHUNK
    lane = lax.broadcasted_iota(jnp.int32, (1, LANES), 1)
    m0 = lane < HEAD_DIM
    ri = lax.broadcasted_iota(jnp.int32, (c2, c2), 0)
    ci = lax.broadcasted_iota(jnp.int32, (c2, c2), 1)
    eye = (ri == ci).astype(F32)
    same_head = (ri // HEAD_DIM) == (ci // HEAD_DIM)
    zero_b = jnp.zeros((), BF16)

    def stack(x):
        z = jnp.zeros_like(x)
        return jnp.concatenate([jnp.where(m0, x, z), jnp.where(m0, z, x)], axis=0)

    def fold(x):
        return x[:RWKV_CHUNK] + x[RWKV_CHUNK:]

    dirs = ((al_f, be_f, ka_f, rh_f, kp_f, bp_f, v_f, pc_f, of_ref, ci < ri, ci <= ri),
            (al_b, be_b, ka_b, rh_b, kp_b, bp_b, v_b, pc_b, ob_ref, ci > ri, ci >= ri))
    units = []
    for d, (al_r, be_r, ka_r, rh_r, kp_r, bp_r, v_r, pc_r, o_r, strict, incl) in enumerate(dirs):
        for g in range(pairs):
            sl = slice(g * LANES, (g + 1) * LANES)
            u = dict(d=d, g=g, sl=sl, o_r=o_r, strict=strict, incl=incl)
            u['al'], u['be'], u['ka'], u['rh'] = al_r[0, :, sl], be_r[0, :, sl], ka_r[0, :, sl], rh_r[0, :, sl]
            u['kp'], u['bp'], u['v'] = kp_r[0, :, sl], bp_r[0, :, sl], v_r[:, sl]
            u['pc'] = pc_r[0, 0, :, sl]
            u['s'] = s_scr[d, g]
            units.append(u)
    for u in units:
        lhs = jnp.concatenate([stack(u['al']), stack(u['rh'])], axis=0)
        u['sb'] = _dot_nt(lhs, stack(u['be']))
        u['sk'] = _dot_nt(lhs, stack(u['ka']))
    for u in units:
        l_k = jnp.where(u['strict'], u['sk'][:c2], 0.0)
        a_k = jnp.where(u['incl'], u['sk'][c2:], 0.0)
        u['a_b'] = fold(jnp.where(u['incl'], u['sb'][c2:], 0.0)).astype(BF16)
        u['lk_ak'] = jnp.concatenate([fold(l_k), fold(a_k)], axis=0).astype(BF16)
        u['x'] = -jnp.where(u['strict'], u['sb'][:c2], 0.0)
        u['t'] = eye + u['x']
    for u in units:
        u['m2'] = _dot_nt(jnp.concatenate([u['al'], u['rh']], axis=0), u['s'].astype(BF16))
        u['m3'] = _dot(u['lk_ak'], stack(u['v']))
    for _ in range(5):
        for u in units:
            xb = u['x'].astype(BF16)
            u['x'] = _dot(xb, xb)
        for u in units:
            u['t'] = u['t'] + _dot(u['t'].astype(BF16), u['x'].astype(BF16))
    for u in units:
        w1 = u['m2'][:RWKV_CHUNK] + u['m3'][:RWKV_CHUNK]
        u['u'] = fold(_dot(u['t'].astype(BF16), stack(w1.astype(BF16))))
    for u in units:
        ub = u['u'].astype(BF16)
        o = u['m2'][RWKV_CHUNK:] + u['m3'][RWKV_CHUNK:] - _dot(u['a_b'], stack(ub))
        u['o_r'][:, u['sl']] = o
        vu_t = jnp.concatenate([u['v'].astype(F32), u['u']], axis=0).T.astype(BF16)
        upd = _dot(vu_t, jnp.concatenate([u['kp'], -u['bp']], axis=0))
        s_scr[u['d'], u['g']] = u['s'] * u['pc'] + jnp.where(same_head, upd, 0.0)

    @pl.when(is_last)
    def _():
        sfin_ref[0] = s_scr[...]


def _rwkv_scan(ops, v, pc, s0, tables, *, pairs):
    m = v.shape[0]
    n_seq = s0.shape[0]
    steps = tables.shape[1]
    gl = pairs * LANES
    ng = RWKV_INNER // gl

    def dspec(d, row):
        return pl.BlockSpec((1, RWKV_CHUNK, gl), lambda g, s, tab: (d, tab[row, s], g))

    def dir_specs(d, row):
        return [dspec(d, row) for _ in range(6)] + [
            pl.BlockSpec((RWKV_CHUNK, gl), lambda g, s, tab: (tab[row, s], g)),
            pl.BlockSpec((1, 1, 1, gl), lambda g, s, tab: (d, tab[row, s], 0, g))]

    in_specs = dir_specs(0, 0) + dir_specs(1, 1) + [
        pl.BlockSpec((1, 2, pairs, LANES, LANES), lambda g, s, tab: (tab[4, s], 0, g, 0, 0))]
    out_specs = [pl.BlockSpec((RWKV_CHUNK, gl), lambda g, s, tab: (tab[0, s], g)),
                 pl.BlockSpec((RWKV_CHUNK, gl), lambda g, s, tab: (tab[1, s], g)),
                 pl.BlockSpec((1, 2, pairs, LANES, LANES), lambda g, s, tab: (tab[4, s], 0, g, 0, 0))]
    sd = jax.ShapeDtypeStruct
    out_shape = [sd((m, RWKV_INNER), F32), sd((m, RWKV_INNER), F32),
                 sd((n_seq, 2, RWKV_HEADS // 2, LANES, LANES), F32)]
    grid_spec = pltpu.PrefetchScalarGridSpec(
        num_scalar_prefetch=1, grid=(ng, steps), in_specs=in_specs, out_specs=out_specs,
        scratch_shapes=[pltpu.VMEM((2, pairs, LANES, LANES), F32)])
    args = list(ops) + [v, pc] + list(ops) + [v, pc] + [s0]
    return pl.pallas_call(
        functools.partial(_rwkv_scan_kernel, pairs=pairs), grid_spec=grid_spec, out_shape=out_shape,
        compiler_params=_cparams(("arbitrary", "arbitrary")), name="rwkv_scan",
    )(tables, *args)


def _rwkv_post_kernel(of_ref, ob_ref, bonus_ref, g_ref, lng_ref, lnb_ref, seg_ref, y_ref):
    o = of_ref[...] + ob_ref[...]
    seg = seg_ref[...]
    inv = 1.0 / HEAD_DIM
    nlb = o.shape[1] // LANES
    for j in range(nlb):
        sl = slice(j * LANES, (j + 1) * LANES)
        oj = o[:, sl]
        mu = _dot_const_rhs2(oj, seg) * inv
        cj = oj - mu
        var = _dot_const_rhs2(cj * cj, seg) * inv
        yj = cj * lax.rsqrt(var + RWKV_LN_EPS) * lng_ref[:, sl] + lnb_ref[:, sl]
        y_ref[:, sl] = ((yj + bonus_ref[:, sl]) * g_ref[:, sl]).astype(y_ref.dtype)


def _rwkv_post(o_f, o_b, bonus, g, ln_g, ln_b, seg, tm):
    m, w = o_f.shape
    tok = lambda: pl.BlockSpec((tm, w), lambda i: (i, 0))
    row = lambda: pl.BlockSpec((1, w), lambda i: (0, 0))
    return pl.pallas_call(
        _rwkv_post_kernel, grid=(m // tm,),
        in_specs=[tok(), tok(), tok(), tok(), row(), row(), pl.BlockSpec((LANES, LANES), lambda i: (0, 0))],
        out_specs=tok(), out_shape=jax.ShapeDtypeStruct((m, w), BF16),
        compiler_params=_cparams(("arbitrary",)), name="rwkv_post",
    )(o_f, o_b, bonus, g, ln_g, ln_b, seg)


def _seg_ones():
    i = np.arange(LANES)
    return jnp.asarray((i[:, None] // HEAD_DIM) == (i[None, :] // HEAD_DIM), BF16)


def _rwkv_consts(tm):
    i = np.arange(tm)
    same = (i[:, None] // RWKV_CHUNK) == (i[None, :] // RWKV_CHUNK)
    tril = same & (i[None, :] <= i[:, None])
    triu = same & (i[None, :] >= i[:, None])
    return jnp.asarray(tril, BF16), jnp.asarray(triu, BF16), jnp.asarray(same, BF16)


def _rwkv_layer_weights(rwkv_mu, w0, w2, a0, a2, g2, kkp, kap, rk, tm):
    inner = RWKV_INNER
    mu = rwkv_mu
    o_wl, o_al, o_gl = 3 * inner, 3 * inner + 128, 3 * inner + 256
    pad_g = 2 * LANES - RWKV_GATE_RANK

    def bd(w):
        z = jnp.zeros_like(w[0])
        return jnp.stack([jnp.concatenate([w[0], z], 0), jnp.concatenate([z, w[1]], 0)]).astype(BF16)

    tril, triu, ones = _rwkv_consts(tm)
    return {
        'mu_r': mu[:, 0:inner], 'mu_k': mu[:, inner:2 * inner], 'mu_v': mu[:, 2 * inner:3 * inner],
        'mu_wl': mu[:, o_wl:o_wl + 128], 'mu_al': mu[:, o_al:o_al + 128],
        'mu_gl': jnp.pad(mu[:, o_gl:], ((0, 0), (0, pad_g))),
        'w0': w0[:, None, :], 'w2': bd(w2), 'a0': a0[:, None, :], 'a2': bd(a2),
        'g2': jnp.pad(g2, ((0, pad_g), (0, 0))).astype(BF16),
        'kk': kkp[None, :], 'ka': kap[None, :], 'rk': rk.reshape(1, inner),
        'tril': tril, 'triu': triu, 'ones': ones, 'seg': _seg_ones(),
    }


def _scan_tables(n_ctx_seq, ctx_chunks, n_lat_seq, lat_chunks):
    fwd, bwd, first, last, seq = [], [], [], [], []
    base = 0
    sid = 0
    for n_seq, n in ((n_ctx_seq, ctx_chunks), (n_lat_seq, lat_chunks)):
        for q in range(n_seq):
            for c in range(n):
                fwd.append(base + c)
                bwd.append(base + n - 1 - c)
                first.append(int(c == 0))
                last.append(int(c == n - 1))
                seq.append(sid)
            base += n
            sid += 1
    return jnp.asarray(np.array([fwd, bwd, first, last, seq], np.int32))


def _pair_blockdiag(s):
    sh = s.shape[:-3]
    h = s.shape[-3]
    s = s.reshape(sh + (h // 2, 2, HEAD_DIM, HEAD_DIM))
    z = jnp.zeros_like(s[..., 0, :, :])
    top = jnp.concatenate([s[..., 0, :, :], z], axis=-1)
    bot = jnp.concatenate([z, s[..., 1, :, :]], axis=-1)
    return jnp.concatenate([top, bot], axis=-2)


def _pair_unblock(s):
    a = s[..., :HEAD_DIM, :HEAD_DIM]
    b = s[..., HEAD_DIM:, HEAD_DIM:]
    out = jnp.stack([a, b], axis=-3)
    return out.reshape(s.shape[:-3] + (2 * s.shape[-3], HEAD_DIM, HEAD_DIM))


def rwkv_mix(p, lw, s0_bd, tables, *, tm, n_ctx_tokens, ctx_len, lat_len, pairs):
    outs = _rwkv_prep(p, lw, tm=tm, n_ctx_tokens=n_ctx_tokens, ctx_len=ctx_len, lat_len=lat_len)
    ops, (v, bonus, g, pc) = outs[:6], outs[6:]
    o_f, o_b, s_fin = _rwkv_scan(ops, v, pc, s0_bd, tables, pairs=pairs)
    y = _rwkv_post(o_f, o_b, bonus, g, lw['ln_g'], lw['ln_b'], lw['seg'], tm)
    return y, s_fin


SSD_BC = SSD_GROUPS * SSD_STATE


def _softplus(x):
    return jnp.maximum(x, 0.0) + jnp.log(1.0 + jnp.exp(-jnp.abs(x)))


def _ssd_prep_kernel(x_ref, x_p, x_n, dt_ref, w_ref, b_ref, dtb_ref, aneg_ref, act_o, dt_o, a_o,
                     *, tm, n_ctx_tokens, ctx_len, lat_len):
    i = pl.program_id(0)
    tok0 = i * tm
    in_ctx = tok0 < n_ctx_tokens
    pos = jnp.where(in_ctx, tok0 % ctx_len, (tok0 - n_ctx_tokens) % lat_len)
    seq_len = jnp.where(in_ctx, ctx_len, lat_len)
    has_prev = pos != 0
    has_next = pos + tm != seq_len
    x = x_ref[...]
    rows = lax.broadcasted_iota(jnp.int32, x.shape, 0)
    p_row = jnp.where(has_prev, x_p[SUBLANES - 1:SUBLANES, :], 0.0)
    n_row = jnp.where(has_next, x_n[0:1, :], 0.0)
    x_prev = jnp.where(rows == 0, p_row, pltpu.roll(x, 1, axis=0))
    x_next = jnp.where(rows == tm - 1, n_row, pltpu.roll(x, tm - 1, axis=0))
    y = x_prev * w_ref[0:1, :] + x * w_ref[1:2, :] + x_next * w_ref[2:3, :] + b_ref[...]
    act_o[...] = _silu(y).astype(act_o.dtype)
    dt = _softplus(dt_ref[...] + dtb_ref[...])
    dt_o[...] = dt
    a_o[...] = dt * aneg_ref[...]


def _ssd_prep(p, conv_w, conv_b, dt_bias, a_neg, *, tm, n_ctx_tokens, ctx_len, lat_len):
    m = p.shape[0]
    hb = tm // SUBLANES
    last8 = m // SUBLANES - 1
    wx = SSD_XBC
    cb = COL_XBC // wx if COL_XBC % wx == 0 else None
    assert cb is not None
    pad = lambda v: jnp.pad(v.reshape(1, -1), ((0, 0), (0, DT_PAD - v.size)))
    kern = functools.partial(_ssd_prep_kernel, tm=tm, n_ctx_tokens=n_ctx_tokens, ctx_len=ctx_len, lat_len=lat_len)
    sd = jax.ShapeDtypeStruct
    return pl.pallas_call(
        kern, grid=(m // tm,),
        in_specs=[pl.BlockSpec((tm, wx), lambda i: (i, cb)),
                  pl.BlockSpec((SUBLANES, wx), lambda i: (jnp.maximum(i * hb - 1, 0), cb)),
                  pl.BlockSpec((SUBLANES, wx), lambda i: (jnp.minimum((i + 1) * hb, last8), cb)),
                  pl.BlockSpec((tm, DT_PAD), lambda i: (i, COL_DT // DT_PAD)),
                  pl.BlockSpec((SSD_CONV, wx), lambda i: (0, 0)), pl.BlockSpec((1, wx), lambda i: (0, 0)),
                  pl.BlockSpec((1, DT_PAD), lambda i: (0, 0)), pl.BlockSpec((1, DT_PAD), lambda i: (0, 0))],
        out_specs=[pl.BlockSpec((tm, wx), lambda i: (i, 0)), pl.BlockSpec((tm, DT_PAD), lambda i: (i, 0)),
                   pl.BlockSpec((tm, DT_PAD), lambda i: (i, 0))],
        out_shape=[sd((m, wx), BF16), sd((m, DT_PAD), F32), sd((m, DT_PAD), F32)],
        compiler_params=_cparams(("arbitrary",)), name="ssd_prep",
    )(p, p, p, p, conv_w, conv_b.reshape(1, wx), pad(dt_bias), pad(a_neg))


def _ssd_scan_kernel(tab_ref, x_f, b_f, c_f, dt_f, a_f, x_b, b_b, c_b, dt_b, a_b, tril_ref, triu_ref,
                     s0_ref, yf_ref, yb_ref, sfin_ref, s_scr):
    s = pl.program_id(0)

    @pl.when(tab_ref[2, s] == 1)
    def _():
        s_scr[...] = s0_ref[0]

    c = SSD_CHUNK
    lane0 = lax.broadcasted_iota(jnp.int32, (1, LANES), 1) < HEAD_DIM
    ri = lax.broadcasted_iota(jnp.int32, (c, c), 0)
    ci = lax.broadcasted_iota(jnp.int32, (c, c), 1)
    npair = SSD_HEADS // 2
    hpg = SSD_HEADS // SSD_GROUPS
    dirs = ((x_f, b_f, c_f, dt_f, a_f, tril_ref, yf_ref, ci <= ri),
            (x_b, b_b, c_b, dt_b, a_b, triu_ref, yb_ref, ci >= ri))
    for d, (x_r, b_r, c_r, dt_r, a_r, tri_r, y_r, causal) in enumerate(dirs):
        cum = _dot_const_lhs3(tri_r[...], a_r[...])
        tot = cum[c - 1:c, :] if d == 0 else cum[0:1, :]
        cum_t = cum.T
        dt_t = dt_r[...].T
        gs = []
        for g in range(SSD_GROUPS):
            gl = slice(g * SSD_STATE, (g + 1) * SSD_STATE)
            gs.append(_dot_nt(c_r[:, gl], b_r[:, gl]))
        for pr in range(npair):
            sl = slice(pr * LANES, (pr + 1) * LANES)
            grp = (2 * pr) // hpg
            gl = slice(grp * SSD_STATE, (grp + 1) * SSD_STATE)
            xp = x_r[:, sl]
            ms, e_in, wgt, dec = [], [], [], []
            for hh in range(2):
                col = d * SSD_HEADS + 2 * pr + hh
                cc = cum[:, col:col + 1]
                diff = jnp.where(causal, cc - cum_t[col:col + 1, :], MASK_BIAS)
                ms.append((gs[grp] * jnp.exp(diff) * dt_t[col:col + 1, :]).astype(BF16))
                e_in.append(jnp.exp(cc))
                wgt.append(jnp.exp(tot[:, col:col + 1] - cc) * dt_r[:, col:col + 1])
                dec.append(jnp.exp(tot[:, col:col + 1]))
            z = jnp.zeros_like(xp)
            x_stack = jnp.concatenate([jnp.where(lane0, xp, z), jnp.where(lane0, z, xp)], axis=0)
            y_diag = _dot(jnp.concatenate(ms, axis=1), x_stack)
            s_pair = s_scr[d, pr]
            y_off = _dot_nt(c_r[:, gl], s_pair.astype(BF16)) * jnp.where(lane0, e_in[0], e_in[1])
            y_r[:, sl] = y_diag + y_off
            xw = xp.astype(F32) * jnp.where(lane0, wgt[0], wgt[1])
            upd = _dot(xw.T.astype(BF16), b_r[:, gl])
            rows_h0 = lax.broadcasted_iota(jnp.int32, (LANES, 1), 0) < HEAD_DIM
            s_scr[d, pr] = s_pair * jnp.where(rows_h0, dec[0], dec[1]) + upd

    @pl.when(tab_ref[3, s] == 1)
    def _():
        sfin_ref[0] = s_scr[...]


def _ssd_scan(act, dt, a, s0, tables):
    m = act.shape[0]
    n_seq = s0.shape[0]
    steps = tables.shape[1]
    c = SSD_CHUNK
    i = np.arange(c)
    tril = jnp.asarray(i[None, :] <= i[:, None], BF16)
    triu = jnp.asarray(i[None, :] >= i[:, None], BF16)
    nxb = SSD_INNER // SSD_BC

    def dir_specs(row):
        return [pl.BlockSpec((c, SSD_INNER), lambda s, tab: (tab[row, s], 0)),
                pl.BlockSpec((c, SSD_BC), lambda s, tab: (tab[row, s], nxb)),
                pl.BlockSpec((c, SSD_BC), lambda s, tab: (tab[row, s], nxb + 1)),
                pl.BlockSpec((c, DT_PAD), lambda s, tab: (tab[row, s], 0)),
                pl.BlockSpec((c, DT_PAD), lambda s, tab: (tab[row, s], 0))]

    st_spec = lambda: pl.BlockSpec((1, 2, SSD_HEADS // 2, LANES, SSD_STATE), lambda s, tab: (tab[4, s], 0, 0, 0, 0))
    in_specs = dir_specs(0) + dir_specs(1) + [pl.BlockSpec((c, c), lambda s, tab: (0, 0)),
                                              pl.BlockSpec((c, c), lambda s, tab: (0, 0)), st_spec()]
    out_specs = [pl.BlockSpec((c, SSD_INNER), lambda s, tab: (tab[0, s], 0)),
                 pl.BlockSpec((c, SSD_INNER), lambda s, tab: (tab[1, s], 0)), st_spec()]
    sd = jax.ShapeDtypeStruct
    out_shape = [sd((m, SSD_INNER), F32), sd((m, SSD_INNER), F32), sd(s0.shape, F32)]
    grid_spec = pltpu.PrefetchScalarGridSpec(
        num_scalar_prefetch=1, grid=(steps,), in_specs=in_specs, out_specs=out_specs,
        scratch_shapes=[pltpu.VMEM(s0.shape[1:], F32)])
    args = [act, act, act, dt, a] * 2 + [tril, triu, s0]
    return pl.pallas_call(
        _ssd_scan_kernel, grid_spec=grid_spec, out_shape=out_shape,
        compiler_params=_cparams(("arbitrary",)), name="ssd_scan",
    )(tables, *args)


def _ssd_post_kernel(yf_ref, yb_ref, x_ref, z_ref, d_ref, g_ref, o_ref):
    y = (x_ref[...].astype(F32) * d_ref[...] + yf_ref[...] + yb_ref[...]) * _silu(z_ref[...])
    gw = SSD_INNER // SSD_GROUPS
    for g in range(SSD_GROUPS):
        sl = slice(g * gw, (g + 1) * gw)
        yg = y[:, sl]
        yg = yg * lax.rsqrt(jnp.mean(yg * yg, axis=-1, keepdims=True) + NORM_EPS)
        o_ref[:, sl] = (yg * g_ref[:, sl]).astype(o_ref.dtype)


def _ssd_post(y_f, y_b, act, p, d_lanes, norm_g, tm):
    m = y_f.shape[0]
    w = SSD_INNER
    tok = lambda: pl.BlockSpec((tm, w), lambda i: (i, 0))
    row = lambda: pl.BlockSpec((1, w), lambda i: (0, 0))
    return pl.pallas_call(
        _ssd_post_kernel, grid=(m // tm,),
        in_specs=[tok(), tok(), tok(), pl.BlockSpec((tm, w), lambda i: (i, COL_Z // w)), row(), row()],
        out_specs=tok(), out_shape=jax.ShapeDtypeStruct((m, w), BF16),
        compiler_params=_cparams(("arbitrary",)), name="ssd_post",
    )(y_f, y_b, act, p, d_lanes, norm_g)


def ssd_mix(p, conv_w, conv_b, dt_bias, a_log, d_skip, norm_g, s0, tables, *, tm, n_ctx_tokens, ctx_len, lat_len):
    a_neg = -jnp.exp(a_log)
    act, dt, a = _ssd_prep(p, conv_w, conv_b, dt_bias, a_neg, tm=tm, n_ctx_tokens=n_ctx_tokens,
                           ctx_len=ctx_len, lat_len=lat_len)
    sh = s0.shape
    s0p = s0.reshape(sh[0], 2, SSD_HEADS // 2, LANES, SSD_STATE)
    y_f, y_b, s_fin = _ssd_scan(act, dt, a, s0p, tables)
    d_lanes = jnp.repeat(d_skip, HEAD_DIM)[None]
    y = _ssd_post(y_f, y_b, act, p, d_lanes, norm_g[None], tm)
    return y, s_fin.reshape(sh)


ATTN_SCALE = HEAD_DIM ** -0.5
MASK_BIAS = -1e30
NA_Q_ROWS = 4
NA_K_ROWS = 12


def _softmax_pv(q_heads, key_sets, lane_head0):
    out = None
    for h, qm in enumerate(q_heads):
        scores = []
        for kb, _, bias in key_sets:
            s = _dot_nt(qm, kb)
            scores.append(s if bias is None else s + bias[h])
        mx = scores[0].max(axis=-1, keepdims=True)
        for s in scores[1:]:
            mx = jnp.maximum(mx, s.max(axis=-1, keepdims=True))
        den = 0.0
        acc = 0.0
        for s, (_, vb, _) in zip(scores, key_sets):
            e = jnp.exp(s - mx)
            den = den + e.sum(axis=-1, keepdims=True)
            acc = acc + _dot(e.astype(BF16), vb)
        o = acc / den
        keep = lane_head0 if h == 0 else jnp.logical_not(lane_head0)
        o = jnp.where(keep, o, 0.0)
        out = o if out is None else out + o
    return out


def _head_queries(q):
    lane0 = lax.broadcasted_iota(jnp.int32, (1, LANES), 1) < HEAD_DIM
    qs = q * ATTN_SCALE
    return [jnp.where(lane0, qs, 0.0).astype(BF16), jnp.where(lane0, 0.0, qs).astype(BF16)], lane0


def _ctx_attn_kernel(q_ref, k_ref, v_ref, o_ref):
    q_heads, lane0 = _head_queries(q_ref[...])
    sets = [(k_ref[...].astype(BF16), v_ref[...].astype(BF16), None)]
    o_ref[...] = _softmax_pv(q_heads, sets, lane0).astype(o_ref.dtype)


def _context_attention(p, n_seq, seq):
    npair = NA_INNER // LANES
    spec = lambda col: pl.BlockSpec((seq, LANES), lambda b, g: (b, col // LANES + g))
    return pl.pallas_call(
        _ctx_attn_kernel, grid=(n_seq, npair),
        in_specs=[spec(COL_Q), spec(COL_K), spec(COL_V)],
        out_specs=pl.BlockSpec((seq, LANES), lambda b, g: (b, g)),
        out_shape=jax.ShapeDtypeStruct((n_seq * seq, NA_INNER), BF16),
        compiler_params=_cparams(("arbitrary", "arbitrary")), name="context_attention",
    )(p, p, p)


def _na_geometry(rows):
    wr = min(WIN_R, rows)
    assert wr == WIN_R and rows % NA_Q_ROWS == 0 and rows >= NA_K_ROWS
    n_rb = rows // NA_Q_ROWS
    r_ids = np.arange(rows)
    key_r0 = np.clip(r_ids - wr // 2, 0, rows - wr)
    starts = np.clip(np.arange(n_rb) * NA_Q_ROWS - wr // 2, 0, rows - NA_K_ROWS)
    sigs, types = [], []
    for rb in range(n_rb):
        rr = np.arange(rb * NA_Q_ROWS, (rb + 1) * NA_Q_ROWS)
        assert starts[rb] <= key_r0[rr].min() and key_r0[rr].max() + wr <= starts[rb] + NA_K_ROWS
        sig = (int(starts[rb] - rb * NA_Q_ROWS),) + tuple(int(x) for x in key_r0[rr] - rb * NA_Q_ROWS)
        if sig not in sigs:
            sigs.append(sig)
        types.append(sigs.index(sig))
    return n_rb, starts.astype(np.int32), np.array(types, np.int32), sigs


def _na_bias_table(rpb, rows):
    _, _, _, sigs = _na_geometry(rows)
    h = rpb.shape[0]
    c_ids = np.arange(GRID_W)
    key_c0 = np.clip(c_ids - WIN_C // 2, 0, GRID_W - WIN_C)
    col_ok = (c_ids[None, :] >= key_c0[:, None]) & (c_ids[None, :] < key_c0[:, None] + WIN_C)
    pad = GRID_W - WIN_C
    rp = jnp.pad(rpb, ((0, 0), (0, 0), (pad, pad)))
    toep = jnp.stack([rp[:, :, GRID_W - 1 - c:2 * GRID_W - 1 - c] for c in range(GRID_W)], axis=2)
    toep = jnp.where(jnp.asarray(col_ok)[None, None], toep, MASK_BIAS)
    masked = jnp.full((h, GRID_W, GRID_W), MASK_BIAS, F32)
    tabs = []
    for sig in sigs:
        start_rel, kr0_rel = sig[0], sig[1:]
        q_rows = []
        for rr in range(NA_Q_ROWS):
            blocks = []
            for kk in range(NA_K_ROWS):
                kr = start_rel + kk
                inside = kr0_rel[rr] <= kr < kr0_rel[rr] + WIN_R
                blocks.append(toep[:, kr - rr + WIN_R - 1] if inside else masked)
            q_rows.append(jnp.concatenate(blocks, axis=-1))
        tabs.append(jnp.concatenate(q_rows, axis=-2))
    return jnp.stack(tabs, axis=1)


def _nbr_attn_kernel(tab_ref, q_ref, k_ref, v_ref, kc_ref, vc_ref, bias_ref, o_ref):
    rb = pl.program_id(2)
    start = pl.multiple_of(tab_ref[0, rb] * GRID_W, GRID_W)
    n_k = NA_K_ROWS * GRID_W
    q_heads, lane0 = _head_queries(q_ref[...])
    sets = [(k_ref[pl.ds(start, n_k), :].astype(BF16), v_ref[pl.ds(start, n_k), :].astype(BF16), bias_ref),
            (kc_ref[0].astype(BF16), vc_ref[0].astype(BF16), None)]
    o_ref[...] = _softmax_pv(q_heads, sets, lane0).astype(o_ref.dtype)


def _neighbourhood_attention(p, k_ctx, v_ctx, bias_tab, row_off_blocks, n_b, t):
    rows = t // GRID_W
    n_rb, starts, types, _ = _na_geometry(rows)
    tq = NA_Q_ROWS * GRID_W
    npair = NA_INNER // LANES
    past = k_ctx.shape[1]
    tabs = jnp.asarray(np.stack([starts, types]))
    qspec = pl.BlockSpec((tq, LANES), lambda b, g, r, tab: ((row_off_blocks + b) * n_rb + r, COL_Q // LANES + g))
    kv = lambda col: pl.BlockSpec((t, LANES), lambda b, g, r, tab: (row_off_blocks + b, col // LANES + g))
    cspec = lambda: pl.BlockSpec((1, past, LANES), lambda b, g, r, tab: (b, 0, g))
    bspec = pl.BlockSpec((2, None, tq, NA_K_ROWS * GRID_W), lambda b, g, r, tab: (g, tab[1, r], 0, 0))
    grid_spec = pltpu.PrefetchScalarGridSpec(
        num_scalar_prefetch=1, grid=(n_b, npair, n_rb),
        in_specs=[qspec, kv(COL_K), kv(COL_V), cspec(), cspec(), bspec],
        out_specs=pl.BlockSpec((tq, LANES), lambda b, g, r, tab: (b * n_rb + r, g)))
    return pl.pallas_call(
        _nbr_attn_kernel, grid_spec=grid_spec,
        out_shape=jax.ShapeDtypeStruct((n_b * t, NA_INNER), BF16),
        compiler_params=_cparams(("arbitrary", "arbitrary", "arbitrary")), name="neighbourhood_attention",
    )(tabs, p, p, p, k_ctx, v_ctx, bias_tab)


def _branch_merge_kernel(y0_ref, y1_ref, y2_ref, w_ref, g0_ref, g1_ref, g2_ref, o_ref):
    acc = _sigmoid(g0_ref[...]) * _dot(y0_ref[...], w_ref[0])
    acc += _sigmoid(g1_ref[...]) * _dot(y1_ref[...], w_ref[1])
    acc += _sigmoid(g2_ref[...]) * _dot(y2_ref[...], w_ref[2])
    o_ref[...] = acc.astype(o_ref.dtype)


def _branch_merge(ys, w_branch, p, tm, tn):
    m, kdim = ys[0].shape
    n = w_branch.shape[2]
    gb = COL_GATE // tn
    yspec = lambda: pl.BlockSpec((tm, kdim), lambda i, j: (i, 0))
    gspec = lambda b: pl.BlockSpec((tm, tn), lambda i, j: (i, gb + b * (n // tn) + j))
    return pl.pallas_call(
        _branch_merge_kernel, grid=(m // tm, n // tn),
        in_specs=[yspec(), yspec(), yspec(), pl.BlockSpec((N_BRANCHES, kdim, tn), lambda i, j: (0, 0, j)),
                  gspec(0), gspec(1), gspec(2)],
        out_specs=pl.BlockSpec((tm, tn), lambda i, j: (i, j)),
        out_shape=jax.ShapeDtypeStruct((m, n), BF16),
        compiler_params=_cparams(("arbitrary", "arbitrary")), name="branch_merge",
    )(ys[0], ys[1], ys[2], w_branch, p, p, p)


def _proj_residual_kernel(a_ref, w_ref, x_ref, mod_ref, o_ref, *, gate_row):
    g = mod_ref[0][gate_row:gate_row + 1, :]
    o_ref[...] = x_ref[...] + g * _dot(a_ref[...], w_ref[...])


def _proj_residual(a, w, x, mod, seg_tokens, gate_row, tm, tn):
    m, kdim = a.shape
    n = w.shape[1]
    return pl.pallas_call(
        functools.partial(_proj_residual_kernel, gate_row=gate_row), grid=(m // tm, n // tn),
        in_specs=[pl.BlockSpec((tm, kdim), lambda i, j: (i, 0)),
                  pl.BlockSpec((kdim, tn), lambda i, j: (0, j)),
                  pl.BlockSpec((tm, tn), lambda i, j: (i, j)),
                  pl.BlockSpec((1, SUBLANES, tn), lambda i, j: ((i * tm) // seg_tokens, 0, j))],
        out_specs=pl.BlockSpec((tm, tn), lambda i, j: (i, j)),
        out_shape=jax.ShapeDtypeStruct((m, n), F32),
        compiler_params=_cparams(("arbitrary", "arbitrary")), name="proj_residual",
    )(a, w, x, mod)


def _final_norm_kernel(x_ref, g_ref, o_ref):
    x = x_ref[...]
    o_ref[...] = x * lax.rsqrt(jnp.mean(x * x, axis=-1, keepdims=True) + NORM_EPS) * g_ref[...]


def _final_norm(x, g, tm):
    m, d = x.shape
    return pl.pallas_call(
        _final_norm_kernel, grid=(m // tm,),
        in_specs=[pl.BlockSpec((tm, d), lambda i: (i, 0)), pl.BlockSpec((1, d), lambda i: (0, 0))],
        out_specs=pl.BlockSpec((tm, d), lambda i: (i, 0)),
        out_shape=jax.ShapeDtypeStruct((m, d), F32),
        compiler_params=_cparams(("arbitrary",)), name="final_norm",
    )(x, g)


MOE_TM = 256
MOE_TN = 1024
NEG_BIG = -3.0e38


def _moe_route_kernel(x_ref, g_ref, mod_ref, wh_ref, wl_ref, b_ref, tril_ref,
                      hn_ref, idx_ref, rank_ref, prob_ref, cnt_ref, carry, *, shift_row, scale_row):
    @pl.when(pl.program_id(0) == 0)
    def _():
        carry[...] = jnp.zeros_like(carry)

    x = x_ref[...]
    y = x * lax.rsqrt(jnp.mean(x * x, axis=-1, keepdims=True) + NORM_EPS) * g_ref[...]
    m = mod_ref[0]
    hn = y * (1.0 + m[scale_row:scale_row + 1, :]) + m[shift_row:shift_row + 1, :]
    hn_ref[...] = hn
    hi = hn.astype(BF16)
    lo = (hn - hi.astype(F32)).astype(BF16)
    wh = wh_ref[...]
    logits = _dot(hi, wh) + _dot(lo, wh) + _dot(hi, wl_ref[...]) + b_ref[...]

    lane_i = lax.broadcasted_iota(jnp.int32, logits.shape, 1)
    lane = lane_i.astype(F32)
    work = logits
    sel = jnp.zeros(logits.shape, jnp.bool_)
    picks = []
    m0 = None
    for k in range(TOP_K):
        mx = work.max(axis=-1, keepdims=True)
        idx = jnp.min(jnp.where(work == mx, lane, float(LANES)), axis=-1, keepdims=True)
        onehot = lane == idx
        sel = jnp.logical_or(sel, onehot)
        work = jnp.where(onehot, NEG_BIG, work)
        picks.append((idx, onehot))
        if k == 0:
            m0 = mx
    e = jnp.where(sel, jnp.exp(logits - m0), 0.0)
    probs = e / e.sum(axis=-1, keepdims=True)

    self_f = jnp.where(sel, 1.0, 0.0)
    before = _dot(tril_ref[...], self_f.astype(BF16)) + carry[...]
    carry[...] = before[-1:, :] + self_f[-1:, :]
    cnt_ref[...] = jnp.broadcast_to(carry[...], cnt_ref.shape)

    idx_o = jnp.zeros(logits.shape, jnp.int32)
    rank_o = jnp.zeros(logits.shape, F32)
    prob_o = jnp.zeros(logits.shape, F32)
    for k, (idx, onehot) in enumerate(picks):
        at_k = lane_i == k
        idx_o = jnp.where(at_k, idx.astype(jnp.int32), idx_o)
        rank_o = jnp.where(at_k, jnp.where(onehot, before, 0.0).sum(axis=-1, keepdims=True), rank_o)
        prob_o = jnp.where(at_k, jnp.where(onehot, probs, 0.0).sum(axis=-1, keepdims=True), prob_o)
    idx_ref[...] = idx_o
    rank_ref[...] = rank_o
    prob_ref[...] = prob_o


def _moe_route(x, g, mod, router_w, router_b, seg_tokens, shift_row, scale_row, tm):
    m, d = x.shape
    ne = router_w.shape[1]
    w = jnp.pad(router_w, ((0, 0), (0, LANES - ne)))
    wh = w.astype(BF16)
    wl = (w - wh.astype(F32)).astype(BF16)
    b = jnp.concatenate([router_b, jnp.full((LANES - ne,), MASK_BIAS, F32)])[None]
    i = np.arange(tm)
    tril = jnp.asarray(i[None, :] < i[:, None], BF16)
    row = lambda: pl.BlockSpec((tm, LANES), lambda t: (t, 0))
    sd = jax.ShapeDtypeStruct
    kern = functools.partial(_moe_route_kernel, shift_row=shift_row, scale_row=scale_row)
    return pl.pallas_call(
        kern, grid=(m // tm,),
        in_specs=[pl.BlockSpec((tm, d), lambda t: (t, 0)), pl.BlockSpec((1, d), lambda t: (0, 0)),
                  pl.BlockSpec((1, SUBLANES, d), lambda t: ((t * tm) // seg_tokens, 0, 0)),
                  pl.BlockSpec((d, LANES), lambda t: (0, 0)), pl.BlockSpec((d, LANES), lambda t: (0, 0)),
                  pl.BlockSpec((1, LANES), lambda t: (0, 0)), pl.BlockSpec((tm, tm), lambda t: (0, 0))],
        out_specs=[pl.BlockSpec((tm, d), lambda t: (t, 0)), row(), row(), row(),
                   pl.BlockSpec((SUBLANES, LANES), lambda t: (0, 0))],
        out_shape=[sd((m, d), F32), sd((m, LANES), jnp.int32), sd((m, LANES), F32), sd((m, LANES), F32),
                   sd((SUBLANES, LANES), F32)],
        scratch_shapes=[pltpu.VMEM((1, LANES), F32)],
        compiler_params=_cparams(("arbitrary",)), name="moe_route",
    )(x, g, mod, wh, wl, b, tril)


def _row_copy(src, dst, s_row, d_row, sem):
    return pltpu.make_async_copy(src.at[pl.ds(s_row, 1)], dst.at[pl.ds(d_row, 1)], sem)


def _moe_dispatch_kernel(slot_ref, hn_ref, zeros_ref, xs_ref, sem, *, tm):
    del zeros_ref

    def issue(t, c):
        for k in range(TOP_K):
            _row_copy(hn_ref, xs_ref, t, slot_ref[t * TOP_K + k], sem).start()
        return c
    lax.fori_loop(0, tm, issue, 0)

    def drain(t, c):
        for k in range(TOP_K):
            _row_copy(hn_ref, xs_ref, t, slot_ref[t * TOP_K + k], sem).wait()
        return c
    lax.fori_loop(0, tm, drain, 0)


def _moe_dispatch(hn, slot_flat, cap, tm):
    m, d = hn.shape
    return pl.pallas_call(
        functools.partial(_moe_dispatch_kernel, tm=tm), grid=(m // tm,),
        in_specs=[pl.BlockSpec((tm * TOP_K,), lambda t: (t,), memory_space=pltpu.SMEM),
                  pl.BlockSpec((tm, d), lambda t: (t, 0)), pl.BlockSpec(memory_space=pl.ANY)],
        out_specs=pl.BlockSpec(memory_space=pl.ANY),
        out_shape=jax.ShapeDtypeStruct((cap, d), F32),
        scratch_shapes=[pltpu.SemaphoreType.DMA(())],
        input_output_aliases={2: 0},
        compiler_params=_cparams(("arbitrary",)), name="moe_dispatch",
    )(slot_flat, hn, jnp.zeros((cap, d), F32))


def _swiglu(gate, up):
    gate = jnp.minimum(gate, SWIGLU_LIMIT)
    up = jnp.clip(up, -SWIGLU_LIMIT, SWIGLU_LIMIT)
    return gate * _sigmoid(gate * SWIGLU_ALPHA) * (up + 1.0)


def _moe_gu_kernel(meta_ref, xs_ref, wg_ref, wu_ref, bg_ref, bu_ref, h_ref, wg_s, wu_s):
    i = pl.program_id(1)

    @pl.when(meta_ref[1, i] == 1)
    def _():
        wg_s[...] = wg_ref[...].astype(BF16)
        wu_s[...] = wu_ref[...].astype(BF16)

    @pl.when(meta_ref[2, i] == 1)
    def _():
        xb = xs_ref[...].astype(BF16)
        gate = _dot(xb, wg_s[...]) + bg_ref[...]
        up = _dot(xb, wu_s[...]) + bu_ref[...]
        h_ref[...] = _swiglu(gate, up).astype(h_ref.dtype)

    @pl.when(meta_ref[2, i] == 0)
    def _():
        h_ref[...] = jnp.zeros_like(h_ref)


def _moe_gu(xs, meta, w_gu, b_gu, layer, tm, tn):
    cap, d = xs.shape
    f = w_gu.shape[3] // 2
    nj = f // tn
    wspec = lambda off: pl.BlockSpec((None, None, d, tn), lambda j, i, mt: (layer, mt[0, i], 0, off + j))
    bspec = lambda off: pl.BlockSpec((None, None, 1, tn), lambda j, i, mt: (layer, mt[0, i], 0, off + j))
    grid_spec = pltpu.PrefetchScalarGridSpec(
        num_scalar_prefetch=1, grid=(nj, cap // tm),
        in_specs=[pl.BlockSpec((tm, d), lambda j, i, mt: (i, 0)), wspec(0), wspec(nj), bspec(0), bspec(nj)],
        out_specs=pl.BlockSpec((tm, tn), lambda j, i, mt: (i, j)),
        scratch_shapes=[pltpu.VMEM((d, tn), BF16), pltpu.VMEM((d, tn), BF16)])
    b4 = b_gu.reshape(b_gu.shape[0], b_gu.shape[1], 1, b_gu.shape[2])
    return pl.pallas_call(
        _moe_gu_kernel, grid_spec=grid_spec, out_shape=jax.ShapeDtypeStruct((cap, f), BF16),
        compiler_params=_cparams(("arbitrary", "arbitrary")), name="moe_gate_up",
    )(meta, xs, w_gu, w_gu, b4, b4)


def _moe_down_kernel(meta_ref, h_ref, w_ref, b_ref, y_ref, w_s):
    i = pl.program_id(1)

    @pl.when(meta_ref[1, i] == 1)
    def _():
        w_s[...] = w_ref[...].astype(BF16)

    @pl.when(meta_ref[2, i] == 1)
    def _():
        y_ref[...] = _dot(h_ref[...], w_s[...]) + b_ref[...]

    @pl.when(meta_ref[2, i] == 0)
    def _():
        y_ref[...] = jnp.zeros_like(y_ref)


def _moe_down(h, meta, w_down, b_down, layer, tm, tn):
    cap, f = h.shape
    d = w_down.shape[3]
    grid_spec = pltpu.PrefetchScalarGridSpec(
        num_scalar_prefetch=1, grid=(d // tn, cap // tm),
        in_specs=[pl.BlockSpec((tm, f), lambda j, i, mt: (i, 0)),
                  pl.BlockSpec((None, None, f, tn), lambda j, i, mt: (layer, mt[0, i], 0, j)),
                  pl.BlockSpec((None, None, 1, tn), lambda j, i, mt: (layer, mt[0, i], 0, j))],
        out_specs=pl.BlockSpec((tm, tn), lambda j, i, mt: (i, j)),
        scratch_shapes=[pltpu.VMEM((f, tn), BF16)])
    b4 = b_down.reshape(b_down.shape[0], b_down.shape[1], 1, b_down.shape[2])
    return pl.pallas_call(
        _moe_down_kernel, grid_spec=grid_spec, out_shape=jax.ShapeDtypeStruct((cap, d), F32),
        compiler_params=_cparams(("arbitrary", "arbitrary")), name="moe_down",
    )(meta, h, w_down, b4)


def _moe_combine_kernel(slot_ref, yb_ref, x_ref, prob_ref, mod_ref, o_ref, buf, sem, *, tm, gate_row):
    def issue(t, c):
        for k in range(TOP_K):
            _row_copy(yb_ref, buf.at[k], slot_ref[t * TOP_K + k], t, sem).start()
        return c
    lax.fori_loop(0, tm, issue, 0)

    def drain(t, c):
        for k in range(TOP_K):
            _row_copy(yb_ref, buf.at[k], slot_ref[t * TOP_K + k], t, sem).wait()
        return c
    lax.fori_loop(0, tm, drain, 0)

    pr = prob_ref[...]
    acc = pr[:, 0:1] * buf[0]
    for k in range(1, TOP_K):
        acc += pr[:, k:k + 1] * buf[k]
    g = mod_ref[0][gate_row:gate_row + 1, :]
    o_ref[...] = x_ref[...] + g * acc


def _moe_combine(yb, slot_flat, prob, x, mod, seg_tokens, gate_row, tm):
    m, d = x.shape
    kern = functools.partial(_moe_combine_kernel, tm=tm, gate_row=gate_row)
    return pl.pallas_call(
        kern, grid=(m // tm,),
        in_specs=[pl.BlockSpec((tm * TOP_K,), lambda t: (t,), memory_space=pltpu.SMEM),
                  pl.BlockSpec(memory_space=pl.ANY),
                  pl.BlockSpec((tm, d), lambda t: (t, 0)),
                  pl.BlockSpec((tm, LANES), lambda t: (t, 0)),
                  pl.BlockSpec((1, SUBLANES, d), lambda t: ((t * tm) // seg_tokens, 0, 0))],
        out_specs=pl.BlockSpec((tm, d), lambda t: (t, 0)),
        out_shape=jax.ShapeDtypeStruct((m, d), F32),
        scratch_shapes=[pltpu.VMEM((TOP_K, tm, d), F32), pltpu.SemaphoreType.DMA(())],
        compiler_params=_cparams(("arbitrary",)), name="moe_combine",
    )(slot_flat, yb, x, prob, mod)


def _moe_layer(x, g, mod, router_w, router_b, w_gu, b_gu, w_down, b_down, layer, seg_tokens):
    m, d = x.shape
    hn, idx, rank, prob, cnt = _moe_route(x, g, mod, router_w, router_b, seg_tokens, 3, 4, MOE_TM)
    counts = cnt[0, :N_EXPERTS].astype(jnp.int32)
    padded = (counts + MOE_TM - 1) // MOE_TM * MOE_TM
    pad_end = jnp.cumsum(padded)
    starts = pad_end - padded
    e4 = idx[:, :TOP_K]
    slot = starts[e4] + rank[:, :TOP_K].astype(jnp.int32)
    slot_flat = slot.reshape(-1)
    n_blocks = (m * TOP_K) // MOE_TM + N_EXPERTS
    cap = n_blocks * MOE_TM
    blk0 = jnp.arange(n_blocks, dtype=jnp.int32) * MOE_TM
    block_e = jnp.minimum(jnp.searchsorted(pad_end, blk0, side='right'), N_EXPERTS - 1).astype(jnp.int32)
    is_new = jnp.concatenate([jnp.ones((1,), jnp.int32), (block_e[1:] != block_e[:-1]).astype(jnp.int32)])
    used = (blk0 < pad_end[-1]).astype(jnp.int32)
    meta = jnp.stack([block_e, is_new, used])
    xs = _moe_dispatch(hn, slot_flat, cap, MOE_TM)
    h = _moe_gu(xs, meta, w_gu, b_gu, layer, MOE_TM, MOE_TN)
    yb = _moe_down(h, meta, w_down, b_down, layer, MOE_TM, MOE_TN)
    return _moe_combine(yb, slot_flat, prob, x, mod, seg_tokens, 5, MOE_TM)


def _split_last(x, sizes):
    cuts = [int(s) for s in np.cumsum(sizes)[:-1]]
    return jnp.split(x, cuts, axis=-1)


def _relayout_w_in(w):
    z, xbc, dt, rw, q, k, v, gate = _split_last(
        w, (SSD_INNER, SSD_XBC, 2 * SSD_HEADS, RWKV_COLS, NA_INNER, NA_INNER, NA_INNER, N_BRANCHES * D_MODEL))
    padc = lambda a, n: jnp.pad(a, ((0, 0), (0, n - a.shape[1])))
    parts = [z, padc(rw, RWKV_PAD), xbc, padc(dt, DT_PAD), q, k, padc(v, COL_GATE - COL_V), gate]
    out = jnp.concatenate(parts, axis=1)
    return padc(out, IN_COLS_PAD).astype(BF16)


def kernel(x_prompt, x_sample, cache_na_k, cache_na_v, state_ssd, state_rwkv, c, c_ctx, norm1_g, norm2_g, w_mod, b_mod, w_in, ssd_conv_w, ssd_conv_b, ssd_dt_bias, ssd_a_log, ssd_d, ssd_norm_g, rwkv_mu, rwkv_w0, rwkv_w2, rwkv_a0, rwkv_a2, rwkv_g2, rwkv_kk, rwkv_ka, rwkv_rk, rwkv_ln_g, rwkv_ln_b, na_rpb, w_branch, w_out, router_w, router_b, moe_w_gu, moe_b_gu, moe_w_down, moe_b_down, final_g):
    nb, seq, d = x_prompt.shape
    db, dseq, _ = x_sample.shape
    n_ctx = nb * seq
    n_lat = db * dseq
    assert n_ctx == dseq, "modulation segments assume context tokens fill exactly one latent-length segment"
    seg_tokens = dseq
    n_seg = 1 + db
    m = n_ctx + n_lat
    rwkv_tm = 256
    tables = _scan_tables(nb, seq // RWKV_CHUNK, db, dseq // RWKV_CHUNK)
    ssd_tables = _scan_tables(nb, seq // SSD_CHUNK, db, dseq // SSD_CHUNK)

    x = jnp.concatenate([x_prompt.reshape(n_ctx, d), x_sample.reshape(n_lat, d)], axis=0)
    cond = jnp.zeros((SUBLANES, d), F32).at[0].set(c_ctx).at[1:1 + db].set(c)

    ctx_k, ctx_v, ctx_ssd, ctx_rwkv = [], [], [], []
    for l in range(DEPTH):
        mod = _modulation(cond, w_mod[l], b_mod[l])
        mod = mod.reshape(SUBLANES, 6, d)[:n_seg]
        mod = jnp.pad(mod, ((0, 0), (0, SUBLANES - 6), (0, 0)))
        p = _norm_matmul(x, norm1_g[l][None], mod, _relayout_w_in(w_in[l]), seg_tokens, 0, 1, 1024, IN_TN)

        lw = _rwkv_layer_weights(rwkv_mu[l], rwkv_w0[l], rwkv_w2[l], rwkv_a0[l], rwkv_a2[l], rwkv_g2[l],
                                 rwkv_kk[l], rwkv_ka[l], rwkv_rk[l], rwkv_tm)
        lw['ln_g'] = rwkv_ln_g[l][None]
        lw['ln_b'] = rwkv_ln_b[l][None]
        s0 = jnp.concatenate([jnp.zeros((nb,) + state_rwkv.shape[2:], F32), state_rwkv[:, l]], axis=0)
        y_rwkv, s_rwkv = rwkv_mix(p, lw, _pair_blockdiag(s0), tables, tm=rwkv_tm, n_ctx_tokens=n_ctx,
                                  ctx_len=seq, lat_len=dseq, pairs=4)
        ctx_rwkv.append(_pair_unblock(s_rwkv[:nb]))

        s0 = jnp.concatenate([jnp.zeros((nb,) + state_ssd.shape[2:], F32), state_ssd[:, l]], axis=0)
        y_ssd, s_ssd = ssd_mix(p, ssd_conv_w[l], ssd_conv_b[l], ssd_dt_bias[l], ssd_a_log[l], ssd_d[l],
                               ssd_norm_g[l], s0, ssd_tables, tm=256, n_ctx_tokens=n_ctx, ctx_len=seq, lat_len=dseq)
        ctx_ssd.append(s_ssd[:nb])

        hd = (NA_HEADS, HEAD_DIM)
        ctx_k.append(p[:n_ctx, COL_K:COL_K + NA_INNER].reshape((nb, seq) + hd))
        ctx_v.append(p[:n_ctx, COL_V:COL_V + NA_INNER].reshape((nb, seq) + hd))
        y_na_c = _context_attention(p, nb, seq)
        past = cache_na_k.shape[2]
        y_na_l = _neighbourhood_attention(p, cache_na_k[:, l].reshape(db, past, NA_INNER),
                                          cache_na_v[:, l].reshape(db, past, NA_INNER),
                                          _na_bias_table(na_rpb[l], dseq // GRID_W), n_ctx // dseq, db, dseq)
        y_na = jnp.concatenate([y_na_c, y_na_l], axis=0)

        merged = _branch_merge((y_ssd, y_rwkv, y_na), w_branch[l].astype(BF16), p, 512, 512)
        x = _proj_residual(merged, w_out[l].astype(BF16), x, mod, seg_tokens, 2, 1024, 512)

        x = _moe_layer(x, norm2_g[l][None], mod, router_w[l], router_b[l], moe_w_gu, moe_b_gu,
                       moe_w_down, moe_b_down, l, seg_tokens)

    y = _final_norm(x, final_g[None], 1024)
    y_prompt = y[:n_ctx].reshape(nb, seq, d)
    y_sample = y[n_ctx:].reshape(db, dseq, d)
    return (y_prompt, y_sample, jnp.stack(ctx_k, axis=1), jnp.stack(ctx_v, axis=1),
            jnp.stack(ctx_ssd, axis=1), jnp.stack(ctx_rwkv, axis=1))
```

```python
import functools
import math

import jax
import jax.numpy as jnp
import numpy as np
from jax import lax
from jax.experimental import pallas as pl
from jax.experimental.pallas import tpu as pltpu

F32 = jnp.float32
BF16 = jnp.bfloat16

D_MODEL = 2048
DEPTH = 4
GRID_W = 64
HEAD_DIM = 64
MIX_WIDTH = D_MODEL // 2
N_BRANCHES = 3
SSD_HEADS = MIX_WIDTH // HEAD_DIM
SSD_INNER = MIX_WIDTH
SSD_GROUPS = 2
SSD_STATE = 128
SSD_CHUNK = 128
SSD_CONV = 3
SSD_XBC = SSD_INNER + 2 * SSD_GROUPS * SSD_STATE
RWKV_HEADS = MIX_WIDTH // HEAD_DIM
RWKV_INNER = MIX_WIDTH
RWKV_DECAY_RANK = 64
RWKV_ICLR_RANK = 64
RWKV_GATE_RANK = 160
RWKV_COLS = 3 * RWKV_INNER + 2 * RWKV_DECAY_RANK + 2 * RWKV_ICLR_RANK + RWKV_GATE_RANK
RWKV_LN_EPS = 64e-5
NA_HEADS = MIX_WIDTH // HEAD_DIM
NA_INNER = MIX_WIDTH
WIN_R = 8
WIN_C = 16
N_EXPERTS = 32
TOP_K = 4
D_FF = D_MODEL
SWIGLU_LIMIT = 7.0
SWIGLU_ALPHA = 1.702
NORM_EPS = 1e-6

LANES = 128
SUBLANES = 8
VMEM_LIMIT = 56 * 1024 * 1024

RWKV_PAD = 3584
DT_PAD = LANES
COL_Z = 0
COL_RWKV = COL_Z + SSD_INNER
COL_XBC = COL_RWKV + RWKV_PAD
COL_DT = COL_XBC + SSD_XBC
COL_Q = COL_DT + DT_PAD
COL_K = COL_Q + NA_INNER
COL_V = COL_K + NA_INNER
IN_TN = 512
COL_QKV_END = COL_V + NA_INNER
COL_GATE = -(-COL_QKV_END // IN_TN) * IN_TN
IN_COLS_USED = COL_GATE + N_BRANCHES * D_MODEL
IN_COLS_PAD = -(-IN_COLS_USED // IN_TN) * IN_TN

RWKV_CHUNK = 64
DECAY_SCALE = math.exp(-0.5)


def _cparams(sem):
    return pltpu.CompilerParams(dimension_semantics=sem, vmem_limit_bytes=VMEM_LIMIT)


def _dot(a, b):
    return jnp.dot(a, b, preferred_element_type=F32)


def _dot_nt(a, b):
    return lax.dot_general(a, b, (((1,), (1,)), ((), ())), preferred_element_type=F32)


def _split3(x):
    hi = x.astype(BF16)
    r1 = x - hi.astype(F32)
    mid = r1.astype(BF16)
    lo = (r1 - mid.astype(F32)).astype(BF16)
    return hi, mid, lo


def _dot_const_lhs3(c_bf16, x):
    hi, mid, lo = _split3(x)
    return _dot(c_bf16, hi) + _dot(c_bf16, mid) + _dot(c_bf16, lo)


def _dot_const_rhs2(x, c_bf16):
    hi = x.astype(BF16)
    lo = (x - hi.astype(F32)).astype(BF16)
    return _dot(hi, c_bf16) + _dot(lo, c_bf16)


def _sigmoid(x):
    return 1.0 / (1.0 + jnp.exp(-x))


def _silu(x):
    return x * _sigmoid(x)


def _mod_kernel(c_ref, w_ref, b_ref, o_ref):
    a = _silu(c_ref[...]).astype(BF16)
    o_ref[...] = _dot(a, w_ref[...].astype(BF16)) + b_ref[...]


def _modulation(cond, w_mod, b_mod):
    n = w_mod.shape[1]
    tn = 1024
    return pl.pallas_call(
        _mod_kernel,
        grid=(n // tn,),
        in_specs=[pl.BlockSpec((SUBLANES, D_MODEL), lambda j: (0, 0)),
                  pl.BlockSpec((D_MODEL, tn), lambda j: (0, j)),
                  pl.BlockSpec((1, tn), lambda j: (0, j))],
        out_specs=pl.BlockSpec((SUBLANES, tn), lambda j: (0, j)),
        out_shape=jax.ShapeDtypeStruct((SUBLANES, n), F32),
        compiler_params=_cparams(("arbitrary",)),
        name="adaln_modulation",
    )(cond, w_mod, b_mod.reshape(1, n))


def _norm_matmul_kernel(x_ref, g_ref, mod_ref, w_ref, o_ref, h_ref, *, shift_row, scale_row):
    @pl.when(pl.program_id(1) == 0)
    def _():
        x = x_ref[...]
        y = x * lax.rsqrt(jnp.mean(x * x, axis=-1, keepdims=True) + NORM_EPS)
        y = y * g_ref[...]
        m = mod_ref[0]
        y = y * (1.0 + m[scale_row:scale_row + 1, :]) + m[shift_row:shift_row + 1, :]
        h_ref[...] = y.astype(BF16)

    o_ref[...] = _dot(h_ref[...], w_ref[...]).astype(o_ref.dtype)


def _norm_matmul(x, g, mod, w, seg_tokens, shift_row, scale_row, tm, tn, out_dtype=F32):
    m, d = x.shape
    n = w.shape[1]
    kern = functools.partial(_norm_matmul_kernel, shift_row=shift_row, scale_row=scale_row)
    return pl.pallas_call(
        kern,
        grid=(m // tm, n // tn),
        in_specs=[pl.BlockSpec((tm, d), lambda i, j: (i, 0)),
                  pl.BlockSpec((1, d), lambda i, j: (0, 0)),
                  pl.BlockSpec((1, SUBLANES, d), lambda i, j: ((i * tm) // seg_tokens, 0, 0)),
                  pl.BlockSpec((d, tn), lambda i, j: (0, j))],
        out_specs=pl.BlockSpec((tm, tn), lambda i, j: (i, j)),
        out_shape=jax.ShapeDtypeStruct((m, n), out_dtype),
        scratch_shapes=[pltpu.VMEM((tm, d), BF16)],
        compiler_params=_cparams(("arbitrary", "arbitrary")),
        name="norm_matmul",
    )(x, g, mod, w)


def _shifted(x, prev8, next8, mu_p, mu_n, has_prev, has_next):
    tm = x.shape[0]
    rows = lax.broadcasted_iota(jnp.int32, x.shape, 0)
    p_row = jnp.where(has_prev, prev8[SUBLANES - 1:SUBLANES, :], 0.0)
    n_row = jnp.where(has_next, next8[0:1, :], 0.0)
    x_prev = jnp.where(rows == 0, p_row, pltpu.roll(x, 1, axis=0))
    x_next = jnp.where(rows == tm - 1, n_row, pltpu.roll(x, tm - 1, axis=0))
    return x + mu_p * (x_prev - x) + mu_n * (x_next - x)


def _rwkv_prep_kernel(
        r_ref, r_p, r_n, k_ref, k_p, k_n, v_ref, v_p, v_n,
        wl_ref, wl_p, wl_n, al_ref, al_p, al_n, gl_ref, gl_p, gl_n,
        mu_r, mu_k, mu_v, mu_wl, mu_al, mu_gl,
        w0_ref, w2_ref, a0_ref, a2_ref, g2_ref, kkp_ref, kap_ref, rk_ref,
        tril_ref, triu_ref, ones_ref, seg_ref,
        alpha_o, beta_o, kappa_o, rho_o, kappa_p_o, beta_p_o, v_o, bonus_o, g_o, pc_o,
        *, tm, n_ctx_tokens, ctx_len, lat_len):
    i = pl.program_id(0)
    tok0 = i * tm
    in_ctx = tok0 < n_ctx_tokens
    pos = jnp.where(in_ctx, tok0 % ctx_len, (tok0 - n_ctx_tokens) % lat_len)
    seq_len = jnp.where(in_ctx, ctx_len, lat_len)
    has_prev = pos != 0
    has_next = pos + tm != seq_len

    def sh(ref, p, n, mu):
        return _shifted(ref[...], p[...], n[...], mu[0:1, :], mu[1:2, :], has_prev, has_next)

    r = sh(r_ref, r_p, r_n, mu_r)
    k = sh(k_ref, k_p, k_n, mu_k)
    v = sh(v_ref, v_p, v_n, mu_v)
    w_lo = sh(wl_ref, wl_p, wl_n, mu_wl)
    a_lo = sh(al_ref, al_p, al_n, mu_al)
    g_lo = sh(gl_ref, gl_p, gl_n, mu_gl)

    seg = seg_ref[...]
    kk = k * kkp_ref[...]
    ss = _dot_const_rhs2(kk * kk, seg)
    kk = kk / jnp.maximum(jnp.sqrt(ss), 1e-12)

    tw = jnp.tanh(w_lo).astype(BF16)
    ab = a_lo.astype(BF16)
    g_o[...] = _dot(_sigmoid(g_lo).astype(BF16), g2_ref[...])
    v_o[...] = v.astype(BF16)

    tri = (tril_ref[...], triu_ref[...])
    ones_blk = ones_ref[...]
    kap = kap_ref[...]
    k_sum = jnp.zeros_like(k)
    nch = tm // RWKV_CHUNK
    for d in range(2):
        logw = w0_ref[d] + _dot(tw, w2_ref[d])
        lw = -DECAY_SCALE * _sigmoid(logw)
        a = _sigmoid(a0_ref[d] + _dot(ab, a2_ref[d]))
        b = kk * a
        k_d = k * (1.0 + (a - 1.0) * kap)
        k_sum = k_sum + k_d
        cum = _dot_const_lhs3(tri[d], lw)
        tot = _dot_const_lhs3(ones_blk, lw)
        e_neg = jnp.exp(-cum)
        e_rest = jnp.exp(tot - cum)
        alpha_o[d] = (kk * jnp.exp(cum - lw)).astype(BF16)
        beta_o[d] = (b * e_neg).astype(BF16)
        kappa_o[d] = (k_d * e_neg).astype(BF16)
        rho_o[d] = (r * jnp.exp(cum)).astype(BF16)
        kappa_p_o[d] = (k_d * e_rest).astype(BF16)
        beta_p_o[d] = (b * e_rest).astype(BF16)
        pcs = jnp.exp(tot)
        for c in range(nch):
            pc_o[d, c] = pcs[c * RWKV_CHUNK:c * RWKV_CHUNK + 1, :]
    bonus_o[...] = _dot_const_rhs2(r * k_sum * rk_ref[...], seg) * v


def _rwkv_prep(p, lw, *, tm, n_ctx_tokens, ctx_len, lat_len):
    m = p.shape[0]
    nlb = RWKV_INNER // LANES
    hb = tm // SUBLANES
    last8 = m // SUBLANES - 1
    cb = COL_RWKV // LANES

    def tile(col_blk, width=LANES, per_j=True):
        wb = width // LANES
        if per_j:
            cm = lambda i, j: (i, col_blk + j)
            cp = lambda i, j: (jnp.maximum(i * hb - 1, 0), col_blk + j)
            cn = lambda i, j: (jnp.minimum((i + 1) * hb, last8), col_blk + j)
        else:
            cm = lambda i, j: (i, col_blk // wb)
            cp = lambda i, j: (jnp.maximum(i * hb - 1, 0), col_blk // wb)
            cn = lambda i, j: (jnp.minimum((i + 1) * hb, last8), col_blk // wb)
        return [pl.BlockSpec((tm, width), cm), pl.BlockSpec((SUBLANES, width), cp),
                pl.BlockSpec((SUBLANES, width), cn)]

    gw = 2 * LANES
    in_specs = (tile(cb) + tile(cb + nlb) + tile(cb + 2 * nlb)
                + tile(cb + 3 * nlb, per_j=False) + tile(cb + 3 * nlb + 1, per_j=False)
                + tile(cb + 3 * nlb + 2, width=gw, per_j=False))
    per_lane = lambda rows: pl.BlockSpec((rows, LANES), lambda i, j: (0, j))
    const2 = lambda a, b: pl.BlockSpec((a, b), lambda i, j: (0, 0))
    in_specs += [per_lane(2), per_lane(2), per_lane(2), const2(2, LANES), const2(2, LANES), const2(2, gw)]
    in_specs += [pl.BlockSpec((2, 1, LANES), lambda i, j: (0, 0, j)),
                 pl.BlockSpec((2, LANES, LANES), lambda i, j: (0, 0, j)),
                 pl.BlockSpec((2, 1, LANES), lambda i, j: (0, 0, j)),
                 pl.BlockSpec((2, LANES, LANES), lambda i, j: (0, 0, j)),
                 pl.BlockSpec((gw, LANES), lambda i, j: (0, j)),
                 per_lane(1), per_lane(1), per_lane(1),
                 const2(tm, tm), const2(tm, tm), const2(tm, tm), const2(LANES, LANES)]
    tok_d = lambda: pl.BlockSpec((2, tm, LANES), lambda i, j: (0, i, j))
    tok = lambda: pl.BlockSpec((tm, LANES), lambda i, j: (i, j))
    nch = tm // RWKV_CHUNK
    out_specs = [tok_d() for _ in range(6)] + [tok(), tok(), tok(),
                 pl.BlockSpec((2, nch, 1, LANES), lambda i, j: (0, i, 0, j))]
    sd = jax.ShapeDtypeStruct
    out_shape = [sd((2, m, RWKV_INNER), BF16) for _ in range(6)] + [
        sd((m, RWKV_INNER), BF16), sd((m, RWKV_INNER), F32), sd((m, RWKV_INNER), F32),
        sd((2, m // RWKV_CHUNK, 1, RWKV_INNER), F32)]
    kern = functools.partial(_rwkv_prep_kernel, tm=tm, n_ctx_tokens=n_ctx_tokens, ctx_len=ctx_len, lat_len=lat_len)
    args = [p] * 18 + [lw['mu_r'], lw['mu_k'], lw['mu_v'], lw['mu_wl'], lw['mu_al'], lw['mu_gl'],
                       lw['w0'], lw['w2'], lw['a0'], lw['a2'], lw['g2'], lw['kk'], lw['ka'], lw['rk'],
                       lw['tril'], lw['triu'], lw['ones'], lw['seg']]
    return pl.pallas_call(
        kern, grid=(m // tm, nlb), in_specs=in_specs, out_specs=out_specs, out_shape=out_shape,
        compiler_params=_cparams(("arbitrary", "arbitrary")), name="rwkv_prep",
    )(*args)


def _rwkv_scan_kernel(tab_ref,
                      al_f, be_f, ka_f, rh_f, kp_f, bp_f, v_f, pc_f,
                      al_b, be_b, ka_b, rh_b, kp_b, bp_b, v_b, pc_b,
                      s0_ref, of_ref, ob_ref, sfin_ref, s_scr, *, pairs):
    s = pl.program_id(1)
    is_first = tab_ref[2, s] == 1
    is_last = tab_ref[3, s] == 1

    @pl.when(is_first)
    def _():
        s_scr[...] = s0_ref[0]

    c2 = 2 * RWKV_CHUNK
    lane = lax.broadcasted_iota(jnp.int32, (1, LANES), 1)
    m0 = lane < HEAD_DIM
    ri = lax.broadcasted_iota(jnp.int32, (c2, c2), 0)
    ci = lax.broadcasted_iota(jnp.int32, (c2, c2), 1)
    eye = (ri == ci).astype(F32)
    same_head = (ri // HEAD_DIM) == (ci // HEAD_DIM)
    zero_b = jnp.zeros((), BF16)

    def stack(x):
        z = jnp.zeros_like(x)
        return jnp.concatenate([jnp.where(m0, x, z), jnp.where(m0, z, x)], axis=0)

    def fold(x):
        return x[:RWKV_CHUNK] + x[RWKV_CHUNK:]

    dirs = ((al_f, be_f, ka_f, rh_f, kp_f, bp_f, v_f, pc_f, of_ref, ci < ri, ci <= ri),
            (al_b, be_b, ka_b, rh_b, kp_b, bp_b, v_b, pc_b, ob_ref, ci > ri, ci >= ri))
    units = []
    for d, (al_r, be_r, ka_r, rh_r, kp_r, bp_r, v_r, pc_r, o_r, strict, incl) in enumerate(dirs):
        for g in range(pairs):
            sl = slice(g * LANES, (g + 1) * LANES)
            u = dict(d=d, g=g, sl=sl, o_r=o_r, strict=strict, incl=incl)
            u['al'], u['be'], u['ka'], u['rh'] = al_r[0, :, sl], be_r[0, :, sl], ka_r[0, :, sl], rh_r[0, :, sl]
            u['kp'], u['bp'], u['v'] = kp_r[0, :, sl], bp_r[0, :, sl], v_r[:, sl]
            u['pc'] = pc_r[0, 0, :, sl]
            u['s'] = s_scr[d, g]
            units.append(u)
    for u in units:
        lhs = jnp.concatenate([stack(u['al']), stack(u['rh'])], axis=0)
        u['sb'] = _dot_nt(lhs, stack(u['be']))
        u['sk'] = _dot_nt(lhs, stack(u['ka']))
    for u in units:
        l_k = jnp.where(u['strict'], u['sk'][:c2], 0.0)
        a_k = jnp.where(u['incl'], u['sk'][c2:], 0.0)
        u['a_b'] = fold(jnp.where(u['incl'], u['sb'][c2:], 0.0)).astype(BF16)
        u['lk_ak'] = jnp.concatenate([fold(l_k), fold(a_k)], axis=0).astype(BF16)
        u['x'] = -jnp.where(u['strict'], u['sb'][:c2], 0.0)
        u['t'] = eye + u['x']
    for u in units:
        u['m2'] = _dot_nt(jnp.concatenate([u['al'], u['rh']], axis=0), u['s'].astype(BF16))
        u['m3'] = _dot(u['lk_ak'], stack(u['v']))
    for _ in range(5):
        for u in units:
            xb = u['x'].astype(BF16)
            u['x'] = _dot(xb, xb)
        for u in units:
            u['t'] = u['t'] + _dot(u['t'].astype(BF16), u['x'].astype(BF16))
    for u in units:
        w1 = u['m2'][:RWKV_CHUNK] + u['m3'][:RWKV_CHUNK]
        u['u'] = fold(_dot(u['t'].astype(BF16), stack(w1.astype(BF16))))
    for u in units:
        ub = u['u'].astype(BF16)
        o = u['m2'][RWKV_CHUNK:] + u['m3'][RWKV_CHUNK:] - _dot(u['a_b'], stack(ub))
        u['o_r'][:, u['sl']] = o
        vu_t = jnp.concatenate([u['v'].astype(F32), u['u']], axis=0).T.astype(BF16)
        upd = _dot(vu_t, jnp.concatenate([u['kp'], -u['bp']], axis=0))
        s_scr[u['d'], u['g']] = u['s'] * u['pc'] + jnp.where(same_head, upd, 0.0)

    @pl.when(is_last)
    def _():
        sfin_ref[0] = s_scr[...]


def _rwkv_scan(ops, v, pc, s0, tables, *, pairs):
    m = v.shape[0]
    n_seq = s0.shape[0]
    steps = tables.shape[1]
    gl = pairs * LANES
    ng = RWKV_INNER // gl

    def dspec(d, row):
        return pl.BlockSpec((1, RWKV_CHUNK, gl), lambda g, s, tab: (d, tab[row, s], g))

    def dir_specs(d, row):
        return [dspec(d, row) for _ in range(6)] + [
            pl.BlockSpec((RWKV_CHUNK, gl), lambda g, s, tab: (tab[row, s], g)),
            pl.BlockSpec((1, 1, 1, gl), lambda g, s, tab: (d, tab[row, s], 0, g))]

    in_specs = dir_specs(0, 0) + dir_specs(1, 1) + [
        pl.BlockSpec((1, 2, pairs, LANES, LANES), lambda g, s, tab: (tab[4, s], 0, g, 0, 0))]
    out_specs = [pl.BlockSpec((RWKV_CHUNK, gl), lambda g, s, tab: (tab[0, s], g)),
                 pl.BlockSpec((RWKV_CHUNK, gl), lambda g, s, tab: (tab[1, s], g)),
                 pl.BlockSpec((1, 2, pairs, LANES, LANES), lambda g, s, tab: (tab[4, s], 0, g, 0, 0))]
    sd = jax.ShapeDtypeStruct
    out_shape = [sd((m, RWKV_INNER), F32), sd((m, RWKV_INNER), F32),
                 sd((n_seq, 2, RWKV_HEADS // 2, LANES, LANES), F32)]
    grid_spec = pltpu.PrefetchScalarGridSpec(
        num_scalar_prefetch=1, grid=(ng, steps), in_specs=in_specs, out_specs=out_specs,
        scratch_shapes=[pltpu.VMEM((2, pairs, LANES, LANES), F32)])
    args = list(ops) + [v, pc] + list(ops) + [v, pc] + [s0]
    return pl.pallas_call(
        functools.partial(_rwkv_scan_kernel, pairs=pairs), grid_spec=grid_spec, out_shape=out_shape,
        compiler_params=_cparams(("arbitrary", "arbitrary")), name="rwkv_scan",
    )(tables, *args)


def _rwkv_post_kernel(of_ref, ob_ref, bonus_ref, g_ref, lng_ref, lnb_ref, seg_ref, y_ref):
    o = of_ref[...] + ob_ref[...]
    seg = seg_ref[...]
    inv = 1.0 / HEAD_DIM
    nlb = o.shape[1] // LANES
    for j in range(nlb):
        sl = slice(j * LANES, (j + 1) * LANES)
        oj = o[:, sl]
        mu = _dot_const_rhs2(oj, seg) * inv
        cj = oj - mu
        var = _dot_const_rhs2(cj * cj, seg) * inv
        yj = cj * lax.rsqrt(var + RWKV_LN_EPS) * lng_ref[:, sl] + lnb_ref[:, sl]
        y_ref[:, sl] = ((yj + bonus_ref[:, sl]) * g_ref[:, sl]).astype(y_ref.dtype)


def _rwkv_post(o_f, o_b, bonus, g, ln_g, ln_b, seg, tm):
    m, w = o_f.shape
    tok = lambda: pl.BlockSpec((tm, w), lambda i: (i, 0))
    row = lambda: pl.BlockSpec((1, w), lambda i: (0, 0))
    return pl.pallas_call(
        _rwkv_post_kernel, grid=(m // tm,),
        in_specs=[tok(), tok(), tok(), tok(), row(), row(), pl.BlockSpec((LANES, LANES), lambda i: (0, 0))],
        out_specs=tok(), out_shape=jax.ShapeDtypeStruct((m, w), BF16),
        compiler_params=_cparams(("arbitrary",)), name="rwkv_post",
    )(o_f, o_b, bonus, g, ln_g, ln_b, seg)


def _seg_ones():
    i = np.arange(LANES)
    return jnp.asarray((i[:, None] // HEAD_DIM) == (i[None, :] // HEAD_DIM), BF16)


def _rwkv_consts(tm):
    i = np.arange(tm)
    same = (i[:, None] // RWKV_CHUNK) == (i[None, :] // RWKV_CHUNK)
    tril = same & (i[None, :] <= i[:, None])
    triu = same & (i[None, :] >= i[:, None])
    return jnp.asarray(tril, BF16), jnp.asarray(triu, BF16), jnp.asarray(same, BF16)


def _rwkv_layer_weights(rwkv_mu, w0, w2, a0, a2, g2, kkp, kap, rk, tm):
    inner = RWKV_INNER
    mu = rwkv_mu
    o_wl, o_al, o_gl = 3 * inner, 3 * inner + 128, 3 * inner + 256
    pad_g = 2 * LANES - RWKV_GATE_RANK

    def bd(w):
        z = jnp.zeros_like(w[0])
        return jnp.stack([jnp.concatenate([w[0], z], 0), jnp.concatenate([z, w[1]], 0)]).astype(BF16)

    tril, triu, ones = _rwkv_consts(tm)
    return {
        'mu_r': mu[:, 0:inner], 'mu_k': mu[:, inner:2 * inner], 'mu_v': mu[:, 2 * inner:3 * inner],
        'mu_wl': mu[:, o_wl:o_wl + 128], 'mu_al': mu[:, o_al:o_al + 128],
        'mu_gl': jnp.pad(mu[:, o_gl:], ((0, 0), (0, pad_g))),
        'w0': w0[:, None, :], 'w2': bd(w2), 'a0': a0[:, None, :], 'a2': bd(a2),
        'g2': jnp.pad(g2, ((0, pad_g), (0, 0))).astype(BF16),
        'kk': kkp[None, :], 'ka': kap[None, :], 'rk': rk.reshape(1, inner),
        'tril': tril, 'triu': triu, 'ones': ones, 'seg': _seg_ones(),
    }


def _scan_tables(n_ctx_seq, ctx_chunks, n_lat_seq, lat_chunks):
    fwd, bwd, first, last, seq = [], [], [], [], []
    base = 0
    sid = 0
    for n_seq, n in ((n_ctx_seq, ctx_chunks), (n_lat_seq, lat_chunks)):
        for q in range(n_seq):
            for c in range(n):
                fwd.append(base + c)
                bwd.append(base + n - 1 - c)
                first.append(int(c == 0))
                last.append(int(c == n - 1))
                seq.append(sid)
            base += n
            sid += 1
    return jnp.asarray(np.array([fwd, bwd, first, last, seq], np.int32))


def _pair_blockdiag(s):
    sh = s.shape[:-3]
    h = s.shape[-3]
    s = s.reshape(sh + (h // 2, 2, HEAD_DIM, HEAD_DIM))
    z = jnp.zeros_like(s[..., 0, :, :])
    top = jnp.concatenate([s[..., 0, :, :], z], axis=-1)
    bot = jnp.concatenate([z, s[..., 1, :, :]], axis=-1)
    return jnp.concatenate([top, bot], axis=-2)


def _pair_unblock(s):
    a = s[..., :HEAD_DIM, :HEAD_DIM]
    b = s[..., HEAD_DIM:, HEAD_DIM:]
    out = jnp.stack([a, b], axis=-3)
    return out.reshape(s.shape[:-3] + (2 * s.shape[-3], HEAD_DIM, HEAD_DIM))


def rwkv_mix(p, lw, s0_bd, tables, *, tm, n_ctx_tokens, ctx_len, lat_len, pairs):
    outs = _rwkv_prep(p, lw, tm=tm, n_ctx_tokens=n_ctx_tokens, ctx_len=ctx_len, lat_len=lat_len)
    ops, (v, bonus, g, pc) = outs[:6], outs[6:]
    o_f, o_b, s_fin = _rwkv_scan(ops, v, pc, s0_bd, tables, pairs=pairs)
    y = _rwkv_post(o_f, o_b, bonus, g, lw['ln_g'], lw['ln_b'], lw['seg'], tm)
    return y, s_fin


SSD_BC = SSD_GROUPS * SSD_STATE


def _softplus(x):
    return jnp.maximum(x, 0.0) + jnp.log(1.0 + jnp.exp(-jnp.abs(x)))


def _ssd_prep_kernel(x_ref, x_p, x_n, dt_ref, w_ref, b_ref, dtb_ref, aneg_ref, act_o, dt_o, a_o,
                     *, tm, n_ctx_tokens, ctx_len, lat_len):
    i = pl.program_id(0)
    tok0 = i * tm
    in_ctx = tok0 < n_ctx_tokens
    pos = jnp.where(in_ctx, tok0 % ctx_len, (tok0 - n_ctx_tokens) % lat_len)
    seq_len = jnp.where(in_ctx, ctx_len, lat_len)
    has_prev = pos != 0
    has_next = pos + tm != seq_len
    x = x_ref[...]
    rows = lax.broadcasted_iota(jnp.int32, x.shape, 0)
    p_row = jnp.where(has_prev, x_p[SUBLANES - 1:SUBLANES, :], 0.0)
    n_row = jnp.where(has_next, x_n[0:1, :], 0.0)
    x_prev = jnp.where(rows == 0, p_row, pltpu.roll(x, 1, axis=0))
    x_next = jnp.where(rows == tm - 1, n_row, pltpu.roll(x, tm - 1, axis=0))
    y = x_prev * w_ref[0:1, :] + x * w_ref[1:2, :] + x_next * w_ref[2:3, :] + b_ref[...]
    act_o[...] = _silu(y).astype(act_o.dtype)
    dt = _softplus(dt_ref[...] + dtb_ref[...])
    dt_o[...] = dt
    a_o[...] = dt * aneg_ref[...]


def _ssd_prep(p, conv_w, conv_b, dt_bias, a_neg, *, tm, n_ctx_tokens, ctx_len, lat_len):
    m = p.shape[0]
    hb = tm // SUBLANES
    last8 = m // SUBLANES - 1
    wx = SSD_XBC
    cb = COL_XBC // wx if COL_XBC % wx == 0 else None
    assert cb is not None
    pad = lambda v: jnp.pad(v.reshape(1, -1), ((0, 0), (0, DT_PAD - v.size)))
    kern = functools.partial(_ssd_prep_kernel, tm=tm, n_ctx_tokens=n_ctx_tokens, ctx_len=ctx_len, lat_len=lat_len)
    sd = jax.ShapeDtypeStruct
    return pl.pallas_call(
        kern, grid=(m // tm,),
        in_specs=[pl.BlockSpec((tm, wx), lambda i: (i, cb)),
                  pl.BlockSpec((SUBLANES, wx), lambda i: (jnp.maximum(i * hb - 1, 0), cb)),
                  pl.BlockSpec((SUBLANES, wx), lambda i: (jnp.minimum((i + 1) * hb, last8), cb)),
                  pl.BlockSpec((tm, DT_PAD), lambda i: (i, COL_DT // DT_PAD)),
                  pl.BlockSpec((SSD_CONV, wx), lambda i: (0, 0)), pl.BlockSpec((1, wx), lambda i: (0, 0)),
                  pl.BlockSpec((1, DT_PAD), lambda i: (0, 0)), pl.BlockSpec((1, DT_PAD), lambda i: (0, 0))],
        out_specs=[pl.BlockSpec((tm, wx), lambda i: (i, 0)), pl.BlockSpec((tm, DT_PAD), lambda i: (i, 0)),
                   pl.BlockSpec((tm, DT_PAD), lambda i: (i, 0))],
        out_shape=[sd((m, wx), BF16), sd((m, DT_PAD), F32), sd((m, DT_PAD), F32)],
        compiler_params=_cparams(("arbitrary",)), name="ssd_prep",
    )(p, p, p, p, conv_w, conv_b.reshape(1, wx), pad(dt_bias), pad(a_neg))


def _ssd_scan_kernel(tab_ref, x_f, b_f, c_f, dt_f, a_f, x_b, b_b, c_b, dt_b, a_b, tril_ref, triu_ref,
                     s0_ref, yf_ref, yb_ref, sfin_ref, s_scr):
    s = pl.program_id(0)

    @pl.when(tab_ref[2, s] == 1)
    def _():
        s_scr[...] = s0_ref[0]

    c = SSD_CHUNK
    lane0 = lax.broadcasted_iota(jnp.int32, (1, LANES), 1) < HEAD_DIM
    ri = lax.broadcasted_iota(jnp.int32, (c, c), 0)
    ci = lax.broadcasted_iota(jnp.int32, (c, c), 1)
    npair = SSD_HEADS // 2
    hpg = SSD_HEADS // SSD_GROUPS
    dirs = ((x_f, b_f, c_f, dt_f, a_f, tril_ref, yf_ref, ci <= ri),
            (x_b, b_b, c_b, dt_b, a_b, triu_ref, yb_ref, ci >= ri))
    for d, (x_r, b_r, c_r, dt_r, a_r, tri_r, y_r, causal) in enumerate(dirs):
        cum = _dot_const_lhs3(tri_r[...], a_r[...])
        tot = cum[c - 1:c, :] if d == 0 else cum[0:1, :]
        cum_t = cum.T
        dt_t = dt_r[...].T
        gs = []
        for g in range(SSD_GROUPS):
            gl = slice(g * SSD_STATE, (g + 1) * SSD_STATE)
            gs.append(_dot_nt(c_r[:, gl], b_r[:, gl]))
        for pr in range(npair):
            sl = slice(pr * LANES, (pr + 1) * LANES)
            grp = (2 * pr) // hpg
            gl = slice(grp * SSD_STATE, (grp + 1) * SSD_STATE)
            xp = x_r[:, sl]
            ms, e_in, wgt, dec = [], [], [], []
            for hh in range(2):
                col = d * SSD_HEADS + 2 * pr + hh
                cc = cum[:, col:col + 1]
                diff = jnp.where(causal, cc - cum_t[col:col + 1, :], MASK_BIAS)
                ms.append((gs[grp] * jnp.exp(diff) * dt_t[col:col + 1, :]).astype(BF16))
                e_in.append(jnp.exp(cc))
                wgt.append(jnp.exp(tot[:, col:col + 1] - cc) * dt_r[:, col:col + 1])
                dec.append(jnp.exp(tot[:, col:col + 1]))
            z = jnp.zeros_like(xp)
            x_stack = jnp.concatenate([jnp.where(lane0, xp, z), jnp.where(lane0, z, xp)], axis=0)
            y_diag = _dot(jnp.concatenate(ms, axis=1), x_stack)
            s_pair = s_scr[d, pr]
            y_off = _dot_nt(c_r[:, gl], s_pair.astype(BF16)) * jnp.where(lane0, e_in[0], e_in[1])
            y_r[:, sl] = y_diag + y_off
            xw = xp.astype(F32) * jnp.where(lane0, wgt[0], wgt[1])
            upd = _dot(xw.T.astype(BF16), b_r[:, gl])
            rows_h0 = lax.broadcasted_iota(jnp.int32, (LANES, 1), 0) < HEAD_DIM
            s_scr[d, pr] = s_pair * jnp.where(rows_h0, dec[0], dec[1]) + upd

    @pl.when(tab_ref[3, s] == 1)
    def _():
        sfin_ref[0] = s_scr[...]


def _ssd_scan(act, dt, a, s0, tables):
    m = act.shape[0]
    n_seq = s0.shape[0]
    steps = tables.shape[1]
    c = SSD_CHUNK
    i = np.arange(c)
    tril = jnp.asarray(i[None, :] <= i[:, None], BF16)
    triu = jnp.asarray(i[None, :] >= i[:, None], BF16)
    nxb = SSD_INNER // SSD_BC

    def dir_specs(row):
        return [pl.BlockSpec((c, SSD_INNER), lambda s, tab: (tab[row, s], 0)),
                pl.BlockSpec((c, SSD_BC), lambda s, tab: (tab[row, s], nxb)),
                pl.BlockSpec((c, SSD_BC), lambda s, tab: (tab[row, s], nxb + 1)),
                pl.BlockSpec((c, DT_PAD), lambda s, tab: (tab[row, s], 0)),
                pl.BlockSpec((c, DT_PAD), lambda s, tab: (tab[row, s], 0))]

    st_spec = lambda: pl.BlockSpec((1, 2, SSD_HEADS // 2, LANES, SSD_STATE), lambda s, tab: (tab[4, s], 0, 0, 0, 0))
    in_specs = dir_specs(0) + dir_specs(1) + [pl.BlockSpec((c, c), lambda s, tab: (0, 0)),
                                              pl.BlockSpec((c, c), lambda s, tab: (0, 0)), st_spec()]
    out_specs = [pl.BlockSpec((c, SSD_INNER), lambda s, tab: (tab[0, s], 0)),
                 pl.BlockSpec((c, SSD_INNER), lambda s, tab: (tab[1, s], 0)), st_spec()]
    sd = jax.ShapeDtypeStruct
    out_shape = [sd((m, SSD_INNER), F32), sd((m, SSD_INNER), F32), sd(s0.shape, F32)]
    grid_spec = pltpu.PrefetchScalarGridSpec(
        num_scalar_prefetch=1, grid=(steps,), in_specs=in_specs, out_specs=out_specs,
        scratch_shapes=[pltpu.VMEM(s0.shape[1:], F32)])
    args = [act, act, act, dt, a] * 2 + [tril, triu, s0]
    return pl.pallas_call(
        _ssd_scan_kernel, grid_spec=grid_spec, out_shape=out_shape,
        compiler_params=_cparams(("arbitrary",)), name="ssd_scan",
    )(tables, *args)


def _ssd_post_kernel(yf_ref, yb_ref, x_ref, z_ref, d_ref, g_ref, o_ref):
    y = (x_ref[...].astype(F32) * d_ref[...] + yf_ref[...] + yb_ref[...]) * _silu(z_ref[...])
    gw = SSD_INNER // SSD_GROUPS
    for g in range(SSD_GROUPS):
        sl = slice(g * gw, (g + 1) * gw)
        yg = y[:, sl]
        yg = yg * lax.rsqrt(jnp.mean(yg * yg, axis=-1, keepdims=True) + NORM_EPS)
        o_ref[:, sl] = (yg * g_ref[:, sl]).astype(o_ref.dtype)


def _ssd_post(y_f, y_b, act, p, d_lanes, norm_g, tm):
    m = y_f.shape[0]
    w = SSD_INNER
    tok = lambda: pl.BlockSpec((tm, w), lambda i: (i, 0))
    row = lambda: pl.BlockSpec((1, w), lambda i: (0, 0))
    return pl.pallas_call(
        _ssd_post_kernel, grid=(m // tm,),
        in_specs=[tok(), tok(), tok(), pl.BlockSpec((tm, w), lambda i: (i, COL_Z // w)), row(), row()],
        out_specs=tok(), out_shape=jax.ShapeDtypeStruct((m, w), BF16),
        compiler_params=_cparams(("arbitrary",)), name="ssd_post",
    )(y_f, y_b, act, p, d_lanes, norm_g)


def ssd_mix(p, conv_w, conv_b, dt_bias, a_log, d_skip, norm_g, s0, tables, *, tm, n_ctx_tokens, ctx_len, lat_len):
    a_neg = -jnp.exp(a_log)
    act, dt, a = _ssd_prep(p, conv_w, conv_b, dt_bias, a_neg, tm=tm, n_ctx_tokens=n_ctx_tokens,
                           ctx_len=ctx_len, lat_len=lat_len)
    sh = s0.shape
    s0p = s0.reshape(sh[0], 2, SSD_HEADS // 2, LANES, SSD_STATE)
    y_f, y_b, s_fin = _ssd_scan(act, dt, a, s0p, tables)
    d_lanes = jnp.repeat(d_skip, HEAD_DIM)[None]
    y = _ssd_post(y_f, y_b, act, p, d_lanes, norm_g[None], tm)
    return y, s_fin.reshape(sh)


ATTN_SCALE = HEAD_DIM ** -0.5
MASK_BIAS = -1e30
NA_Q_ROWS = 4
NA_K_ROWS = 12


def _softmax_pv(q_heads, key_sets, lane_head0):
    out = None
    for h, qm in enumerate(q_heads):
        scores = []
        for kb, _, bias in key_sets:
            s = _dot_nt(qm, kb)
            scores.append(s if bias is None else s + bias[h])
        mx = scores[0].max(axis=-1, keepdims=True)
        for s in scores[1:]:
            mx = jnp.maximum(mx, s.max(axis=-1, keepdims=True))
        den = 0.0
        acc = 0.0
        for s, (_, vb, _) in zip(scores, key_sets):
            e = jnp.exp(s - mx)
            den = den + e.sum(axis=-1, keepdims=True)
            acc = acc + _dot(e.astype(BF16), vb)
        o = acc / den
        keep = lane_head0 if h == 0 else jnp.logical_not(lane_head0)
        o = jnp.where(keep, o, 0.0)
        out = o if out is None else out + o
    return out


def _head_queries(q):
    lane0 = lax.broadcasted_iota(jnp.int32, (1, LANES), 1) < HEAD_DIM
    qs = q * ATTN_SCALE
    return [jnp.where(lane0, qs, 0.0).astype(BF16), jnp.where(lane0, 0.0, qs).astype(BF16)], lane0


def _ctx_attn_kernel(q_ref, k_ref, v_ref, o_ref):
    q_heads, lane0 = _head_queries(q_ref[...])
    sets = [(k_ref[...].astype(BF16), v_ref[...].astype(BF16), None)]
    o_ref[...] = _softmax_pv(q_heads, sets, lane0).astype(o_ref.dtype)


def _context_attention(p, n_seq, seq):
    npair = NA_INNER // LANES
    spec = lambda col: pl.BlockSpec((seq, LANES), lambda b, g: (b, col // LANES + g))
    return pl.pallas_call(
        _ctx_attn_kernel, grid=(n_seq, npair),
        in_specs=[spec(COL_Q), spec(COL_K), spec(COL_V)],
        out_specs=pl.BlockSpec((seq, LANES), lambda b, g: (b, g)),
        out_shape=jax.ShapeDtypeStruct((n_seq * seq, NA_INNER), BF16),
        compiler_params=_cparams(("arbitrary", "arbitrary")), name="context_attention",
    )(p, p, p)


def _na_geometry(rows):
    wr = min(WIN_R, rows)
    assert wr == WIN_R and rows % NA_Q_ROWS == 0 and rows >= NA_K_ROWS
    n_rb = rows // NA_Q_ROWS
    r_ids = np.arange(rows)
    key_r0 = np.clip(r_ids - wr // 2, 0, rows - wr)
    starts = np.clip(np.arange(n_rb) * NA_Q_ROWS - wr // 2, 0, rows - NA_K_ROWS)
    sigs, types = [], []
    for rb in range(n_rb):
        rr = np.arange(rb * NA_Q_ROWS, (rb + 1) * NA_Q_ROWS)
        assert starts[rb] <= key_r0[rr].min() and key_r0[rr].max() + wr <= starts[rb] + NA_K_ROWS
        sig = (int(starts[rb] - rb * NA_Q_ROWS),) + tuple(int(x) for x in key_r0[rr] - rb * NA_Q_ROWS)
        if sig not in sigs:
            sigs.append(sig)
        types.append(sigs.index(sig))
    return n_rb, starts.astype(np.int32), np.array(types, np.int32), sigs


def _na_bias_table(rpb, rows):
    _, _, _, sigs = _na_geometry(rows)
    h = rpb.shape[0]
    c_ids = np.arange(GRID_W)
    key_c0 = np.clip(c_ids - WIN_C // 2, 0, GRID_W - WIN_C)
    col_ok = (c_ids[None, :] >= key_c0[:, None]) & (c_ids[None, :] < key_c0[:, None] + WIN_C)
    pad = GRID_W - WIN_C
    rp = jnp.pad(rpb, ((0, 0), (0, 0), (pad, pad)))
    toep = jnp.stack([rp[:, :, GRID_W - 1 - c:2 * GRID_W - 1 - c] for c in range(GRID_W)], axis=2)
    toep = jnp.where(jnp.asarray(col_ok)[None, None], toep, MASK_BIAS)
    masked = jnp.full((h, GRID_W, GRID_W), MASK_BIAS, F32)
    tabs = []
    for sig in sigs:
        start_rel, kr0_rel = sig[0], sig[1:]
        q_rows = []
        for rr in range(NA_Q_ROWS):
            blocks = []
            for kk in range(NA_K_ROWS):
                kr = start_rel + kk
                inside = kr0_rel[rr] <= kr < kr0_rel[rr] + WIN_R
                blocks.append(toep[:, kr - rr + WIN_R - 1] if inside else masked)
            q_rows.append(jnp.concatenate(blocks, axis=-1))
        tabs.append(jnp.concatenate(q_rows, axis=-2))
    return jnp.stack(tabs, axis=1)


def _nbr_attn_kernel(tab_ref, q_ref, k_ref, v_ref, kc_ref, vc_ref, bias_ref, o_ref):
    rb = pl.program_id(2)
    start = pl.multiple_of(tab_ref[0, rb] * GRID_W, GRID_W)
    n_k = NA_K_ROWS * GRID_W
    q_heads, lane0 = _head_queries(q_ref[...])
    sets = [(k_ref[pl.ds(start, n_k), :].astype(BF16), v_ref[pl.ds(start, n_k), :].astype(BF16), bias_ref),
            (kc_ref[0].astype(BF16), vc_ref[0].astype(BF16), None)]
    o_ref[...] = _softmax_pv(q_heads, sets, lane0).astype(o_ref.dtype)


def _neighbourhood_attention(p, k_ctx, v_ctx, bias_tab, row_off_blocks, n_b, t):
    rows = t // GRID_W
    n_rb, starts, types, _ = _na_geometry(rows)
    tq = NA_Q_ROWS * GRID_W
    npair = NA_INNER // LANES
    past = k_ctx.shape[1]
    tabs = jnp.asarray(np.stack([starts, types]))
    qspec = pl.BlockSpec((tq, LANES), lambda b, g, r, tab: ((row_off_blocks + b) * n_rb + r, COL_Q // LANES + g))
    kv = lambda col: pl.BlockSpec((t, LANES), lambda b, g, r, tab: (row_off_blocks + b, col // LANES + g))
    cspec = lambda: pl.BlockSpec((1, past, LANES), lambda b, g, r, tab: (b, 0, g))
    bspec = pl.BlockSpec((2, None, tq, NA_K_ROWS * GRID_W), lambda b, g, r, tab: (g, tab[1, r], 0, 0))
    grid_spec = pltpu.PrefetchScalarGridSpec(
        num_scalar_prefetch=1, grid=(n_b, npair, n_rb),
        in_specs=[qspec, kv(COL_K), kv(COL_V), cspec(), cspec(), bspec],
        out_specs=pl.BlockSpec((tq, LANES), lambda b, g, r, tab: (b * n_rb + r, g)))
    return pl.pallas_call(
        _nbr_attn_kernel, grid_spec=grid_spec,
        out_shape=jax.ShapeDtypeStruct((n_b * t, NA_INNER), BF16),
        compiler_params=_cparams(("arbitrary", "arbitrary", "arbitrary")), name="neighbourhood_attention",
    )(tabs, p, p, p, k_ctx, v_ctx, bias_tab)


def _branch_merge_kernel(y0_ref, y1_ref, y2_ref, w_ref, g0_ref, g1_ref, g2_ref, o_ref):
    acc = _sigmoid(g0_ref[...]) * _dot(y0_ref[...], w_ref[0])
    acc += _sigmoid(g1_ref[...]) * _dot(y1_ref[...], w_ref[1])
    acc += _sigmoid(g2_ref[...]) * _dot(y2_ref[...], w_ref[2])
    o_ref[...] = acc.astype(o_ref.dtype)


def _branch_merge(ys, w_branch, p, tm, tn):
    m, kdim = ys[0].shape
    n = w_branch.shape[2]
    gb = COL_GATE // tn
    yspec = lambda: pl.BlockSpec((tm, kdim), lambda i, j: (i, 0))
    gspec = lambda b: pl.BlockSpec((tm, tn), lambda i, j: (i, gb + b * (n // tn) + j))
    return pl.pallas_call(
        _branch_merge_kernel, grid=(m // tm, n // tn),
        in_specs=[yspec(), yspec(), yspec(), pl.BlockSpec((N_BRANCHES, kdim, tn), lambda i, j: (0, 0, j)),
                  gspec(0), gspec(1), gspec(2)],
        out_specs=pl.BlockSpec((tm, tn), lambda i, j: (i, j)),
        out_shape=jax.ShapeDtypeStruct((m, n), BF16),
        compiler_params=_cparams(("arbitrary", "arbitrary")), name="branch_merge",
    )(ys[0], ys[1], ys[2], w_branch, p, p, p)


def _proj_residual_kernel(a_ref, w_ref, x_ref, mod_ref, o_ref, *, gate_row):
    g = mod_ref[0][gate_row:gate_row + 1, :]
    o_ref[...] = x_ref[...] + g * _dot(a_ref[...], w_ref[...])


def _proj_residual(a, w, x, mod, seg_tokens, gate_row, tm, tn):
    m, kdim = a.shape
    n = w.shape[1]
    return pl.pallas_call(
        functools.partial(_proj_residual_kernel, gate_row=gate_row), grid=(m // tm, n // tn),
        in_specs=[pl.BlockSpec((tm, kdim), lambda i, j: (i, 0)),
                  pl.BlockSpec((kdim, tn), lambda i, j: (0, j)),
                  pl.BlockSpec((tm, tn), lambda i, j: (i, j)),
                  pl.BlockSpec((1, SUBLANES, tn), lambda i, j: ((i * tm) // seg_tokens, 0, j))],
        out_specs=pl.BlockSpec((tm, tn), lambda i, j: (i, j)),
        out_shape=jax.ShapeDtypeStruct((m, n), F32),
        compiler_params=_cparams(("arbitrary", "arbitrary")), name="proj_residual",
    )(a, w, x, mod)


def _final_norm_kernel(x_ref, g_ref, o_ref):
    x = x_ref[...]
    o_ref[...] = x * lax.rsqrt(jnp.mean(x * x, axis=-1, keepdims=True) + NORM_EPS) * g_ref[...]


def _final_norm(x, g, tm):
    m, d = x.shape
    return pl.pallas_call(
        _final_norm_kernel, grid=(m // tm,),
        in_specs=[pl.BlockSpec((tm, d), lambda i: (i, 0)), pl.BlockSpec((1, d), lambda i: (0, 0))],
        out_specs=pl.BlockSpec((tm, d), lambda i: (i, 0)),
        out_shape=jax.ShapeDtypeStruct((m, d), F32),
        compiler_params=_cparams(("arbitrary",)), name="final_norm",
    )(x, g)


MOE_TM = 512
MOE_TOK_TM = 256
MOE_GU_TN = 512
MOE_DOWN_TN = 1024
NEG_BIG = -3.0e38


def _moe_route_kernel(x_ref, g_ref, mod_ref, wh_ref, wl_ref, b_ref, tril_ref,
                      hn_ref, idx_ref, rank_ref, prob_ref, cnt_ref, carry, *, shift_row, scale_row):
    @pl.when(pl.program_id(0) == 0)
    def _():
        carry[...] = jnp.zeros_like(carry)

    x = x_ref[...]
    y = x * lax.rsqrt(jnp.mean(x * x, axis=-1, keepdims=True) + NORM_EPS) * g_ref[...]
    m = mod_ref[0]
    hn = y * (1.0 + m[scale_row:scale_row + 1, :]) + m[shift_row:shift_row + 1, :]
    hn_ref[...] = hn
    hi = hn.astype(BF16)
    lo = (hn - hi.astype(F32)).astype(BF16)
    wh = wh_ref[...]
    logits = _dot(hi, wh) + _dot(lo, wh) + _dot(hi, wl_ref[...]) + b_ref[...]

    lane_i = lax.broadcasted_iota(jnp.int32, logits.shape, 1)
    lane = lane_i.astype(F32)
    work = logits
    sel = jnp.zeros(logits.shape, jnp.bool_)
    picks = []
    m0 = None
    for k in range(TOP_K):
        mx = work.max(axis=-1, keepdims=True)
        idx = jnp.min(jnp.where(work == mx, lane, float(LANES)), axis=-1, keepdims=True)
        onehot = lane == idx
        sel = jnp.logical_or(sel, onehot)
        work = jnp.where(onehot, NEG_BIG, work)
        picks.append((idx, onehot))
        if k == 0:
            m0 = mx
    e = jnp.where(sel, jnp.exp(logits - m0), 0.0)
    probs = e / e.sum(axis=-1, keepdims=True)

    self_f = jnp.where(sel, 1.0, 0.0)
    before = _dot(tril_ref[...], self_f.astype(BF16)) + carry[...]
    carry[...] = before[-1:, :] + self_f[-1:, :]
    cnt_ref[...] = jnp.broadcast_to(carry[...], cnt_ref.shape)

    idx_o = jnp.zeros(logits.shape, jnp.int32)
    rank_o = jnp.zeros(logits.shape, F32)
    prob_o = jnp.zeros(logits.shape, F32)
    for k, (idx, onehot) in enumerate(picks):
        at_k = lane_i == k
        idx_o = jnp.where(at_k, idx.astype(jnp.int32), idx_o)
        rank_o = jnp.where(at_k, jnp.where(onehot, before, 0.0).sum(axis=-1, keepdims=True), rank_o)
        prob_o = jnp.where(at_k, jnp.where(onehot, probs, 0.0).sum(axis=-1, keepdims=True), prob_o)
    idx_ref[...] = idx_o
    rank_ref[...] = rank_o
    prob_ref[...] = prob_o


def _moe_route(x, g, mod, router_w, router_b, seg_tokens, shift_row, scale_row, tm):
    m, d = x.shape
    ne = router_w.shape[1]
    w = jnp.pad(router_w, ((0, 0), (0, LANES - ne)))
    wh = w.astype(BF16)
    wl = (w - wh.astype(F32)).astype(BF16)
    b = jnp.concatenate([router_b, jnp.full((LANES - ne,), MASK_BIAS, F32)])[None]
    i = np.arange(tm)
    tril = jnp.asarray(i[None, :] < i[:, None], BF16)
    row = lambda: pl.BlockSpec((tm, LANES), lambda t: (t, 0))
    sd = jax.ShapeDtypeStruct
    kern = functools.partial(_moe_route_kernel, shift_row=shift_row, scale_row=scale_row)
    return pl.pallas_call(
        kern, grid=(m // tm,),
        in_specs=[pl.BlockSpec((tm, d), lambda t: (t, 0)), pl.BlockSpec((1, d), lambda t: (0, 0)),
                  pl.BlockSpec((1, SUBLANES, d), lambda t: ((t * tm) // seg_tokens, 0, 0)),
                  pl.BlockSpec((d, LANES), lambda t: (0, 0)), pl.BlockSpec((d, LANES), lambda t: (0, 0)),
                  pl.BlockSpec((1, LANES), lambda t: (0, 0)), pl.BlockSpec((tm, tm), lambda t: (0, 0))],
        out_specs=[pl.BlockSpec((tm, d), lambda t: (t, 0)), row(), row(), row(),
                   pl.BlockSpec((SUBLANES, LANES), lambda t: (0, 0))],
        out_shape=[sd((m, d), F32), sd((m, LANES), jnp.int32), sd((m, LANES), F32), sd((m, LANES), F32),
                   sd((SUBLANES, LANES), F32)],
        scratch_shapes=[pltpu.VMEM((1, LANES), F32)],
        compiler_params=_cparams(("arbitrary",)), name="moe_route",
    )(x, g, mod, wh, wl, b, tril)


def _row_copy(src, dst, s_row, d_row, sem):
    return pltpu.make_async_copy(src.at[pl.ds(s_row, 1)], dst.at[pl.ds(d_row, 1)], sem)


def _moe_dispatch_kernel(slot_ref, hn_ref, zeros_ref, xs_ref, sem, *, tm):
    del zeros_ref

    def issue(t, c):
        for k in range(TOP_K):
            _row_copy(hn_ref, xs_ref, t, slot_ref[t * TOP_K + k], sem).start()
        return c
    lax.fori_loop(0, tm, issue, 0)

    def drain(t, c):
        for k in range(TOP_K):
            _row_copy(hn_ref, xs_ref, t, slot_ref[t * TOP_K + k], sem).wait()
        return c
    lax.fori_loop(0, tm, drain, 0)


def _moe_dispatch(hn, slot_flat, cap, tm):
    m, d = hn.shape
    return pl.pallas_call(
        functools.partial(_moe_dispatch_kernel, tm=tm), grid=(m // tm,),
        in_specs=[pl.BlockSpec((tm * TOP_K,), lambda t: (t,), memory_space=pltpu.SMEM),
                  pl.BlockSpec((tm, d), lambda t: (t, 0)), pl.BlockSpec(memory_space=pl.ANY)],
        out_specs=pl.BlockSpec(memory_space=pl.ANY),
        out_shape=jax.ShapeDtypeStruct((cap, d), F32),
        scratch_shapes=[pltpu.SemaphoreType.DMA(())],
        input_output_aliases={2: 0},
        compiler_params=_cparams(("arbitrary",)), name="moe_dispatch",
    )(slot_flat, hn, jnp.zeros((cap, d), F32))


def _swiglu(gate, up):
    gate = jnp.minimum(gate, SWIGLU_LIMIT)
    up = jnp.clip(up, -SWIGLU_LIMIT, SWIGLU_LIMIT)
    return gate * _sigmoid(gate * SWIGLU_ALPHA) * (up + 1.0)


def _moe_gu_kernel(meta_ref, xs_ref, wg_ref, wu_ref, bg_ref, bu_ref, h_ref, wg_s, wu_s):
    i = pl.program_id(1)

    @pl.when(meta_ref[1, i] == 1)
    def _():
        wg_s[...] = wg_ref[...].astype(BF16)
        wu_s[...] = wu_ref[...].astype(BF16)

    @pl.when(meta_ref[2, i] == 1)
    def _():
        xb = xs_ref[...].astype(BF16)
        gate = _dot(xb, wg_s[...]) + bg_ref[...]
        up = _dot(xb, wu_s[...]) + bu_ref[...]
        h_ref[...] = _swiglu(gate, up).astype(h_ref.dtype)

    @pl.when(meta_ref[2, i] == 0)
    def _():
        h_ref[...] = jnp.zeros_like(h_ref)


def _moe_gu(xs, meta, w_gu, b_gu, layer, tm, tn):
    cap, d = xs.shape
    f = w_gu.shape[3] // 2
    nj = f // tn
    wspec = lambda off: pl.BlockSpec((None, None, d, tn), lambda j, i, mt: (layer, mt[0, i], 0, off + j))
    bspec = lambda off: pl.BlockSpec((None, None, 1, tn), lambda j, i, mt: (layer, mt[0, i], 0, off + j))
    grid_spec = pltpu.PrefetchScalarGridSpec(
        num_scalar_prefetch=1, grid=(nj, cap // tm),
        in_specs=[pl.BlockSpec((tm, d), lambda j, i, mt: (i, 0)), wspec(0), wspec(nj), bspec(0), bspec(nj)],
        out_specs=pl.BlockSpec((tm, tn), lambda j, i, mt: (i, j)),
        scratch_shapes=[pltpu.VMEM((d, tn), BF16), pltpu.VMEM((d, tn), BF16)])
    b4 = b_gu.reshape(b_gu.shape[0], b_gu.shape[1], 1, b_gu.shape[2])
    return pl.pallas_call(
        _moe_gu_kernel, grid_spec=grid_spec, out_shape=jax.ShapeDtypeStruct((cap, f), BF16),
        compiler_params=_cparams(("arbitrary", "arbitrary")), name="moe_gate_up",
    )(meta, xs, w_gu, w_gu, b4, b4)


def _moe_down_kernel(meta_ref, h_ref, w_ref, b_ref, y_ref, w_s):
    i = pl.program_id(1)

    @pl.when(meta_ref[1, i] == 1)
    def _():
        w_s[...] = w_ref[...].astype(BF16)

    @pl.when(meta_ref[2, i] == 1)
    def _():
        y_ref[...] = _dot(h_ref[...], w_s[...]) + b_ref[...]

    @pl.when(meta_ref[2, i] == 0)
    def _():
        y_ref[...] = jnp.zeros_like(y_ref)


def _moe_down(h, meta, w_down, b_down, layer, tm, tn):
    cap, f = h.shape
    d = w_down.shape[3]
    grid_spec = pltpu.PrefetchScalarGridSpec(
        num_scalar_prefetch=1, grid=(d // tn, cap // tm),
        in_specs=[pl.BlockSpec((tm, f), lambda j, i, mt: (i, 0)),
                  pl.BlockSpec((None, None, f, tn), lambda j, i, mt: (layer, mt[0, i], 0, j)),
                  pl.BlockSpec((None, None, 1, tn), lambda j, i, mt: (layer, mt[0, i], 0, j))],
        out_specs=pl.BlockSpec((tm, tn), lambda j, i, mt: (i, j)),
        scratch_shapes=[pltpu.VMEM((f, tn), BF16)])
    b4 = b_down.reshape(b_down.shape[0], b_down.shape[1], 1, b_down.shape[2])
    return pl.pallas_call(
        _moe_down_kernel, grid_spec=grid_spec, out_shape=jax.ShapeDtypeStruct((cap, d), F32),
        compiler_params=_cparams(("arbitrary", "arbitrary")), name="moe_down",
    )(meta, h, w_down, b4)


def _moe_combine_kernel(slot_ref, yb_ref, x_ref, prob_ref, mod_ref, o_ref, buf, sem, *, tm, gate_row):
    def issue(t, c):
        for k in range(TOP_K):
            _row_copy(yb_ref, buf.at[k], slot_ref[t * TOP_K + k], t, sem).start()
        return c
    lax.fori_loop(0, tm, issue, 0)

    def drain(t, c):
        for k in range(TOP_K):
            _row_copy(yb_ref, buf.at[k], slot_ref[t * TOP_K + k], t, sem).wait()
        return c
    lax.fori_loop(0, tm, drain, 0)

    pr = prob_ref[...]
    acc = pr[:, 0:1] * buf[0]
    for k in range(1, TOP_K):
        acc += pr[:, k:k + 1] * buf[k]
    g = mod_ref[0][gate_row:gate_row + 1, :]
    o_ref[...] = x_ref[...] + g * acc


def _moe_combine(yb, slot_flat, prob, x, mod, seg_tokens, gate_row, tm):
    m, d = x.shape
    kern = functools.partial(_moe_combine_kernel, tm=tm, gate_row=gate_row)
    return pl.pallas_call(
        kern, grid=(m // tm,),
        in_specs=[pl.BlockSpec((tm * TOP_K,), lambda t: (t,), memory_space=pltpu.SMEM),
                  pl.BlockSpec(memory_space=pl.ANY),
                  pl.BlockSpec((tm, d), lambda t: (t, 0)),
                  pl.BlockSpec((tm, LANES), lambda t: (t, 0)),
                  pl.BlockSpec((1, SUBLANES, d), lambda t: ((t * tm) // seg_tokens, 0, 0))],
        out_specs=pl.BlockSpec((tm, d), lambda t: (t, 0)),
        out_shape=jax.ShapeDtypeStruct((m, d), F32),
        scratch_shapes=[pltpu.VMEM((TOP_K, tm, d), F32), pltpu.SemaphoreType.DMA(())],
        compiler_params=_cparams(("arbitrary",)), name="moe_combine",
    )(slot_flat, yb, x, prob, mod)


def _moe_layer(x, g, mod, router_w, router_b, w_gu, b_gu, w_down, b_down, layer, seg_tokens):
    m, d = x.shape
    hn, idx, rank, prob, cnt = _moe_route(x, g, mod, router_w, router_b, seg_tokens, 3, 4, MOE_TOK_TM)
    counts = cnt[0, :N_EXPERTS].astype(jnp.int32)
    padded = (counts + MOE_TM - 1) // MOE_TM * MOE_TM
    pad_end = jnp.cumsum(padded)
    starts = pad_end - padded
    e4 = idx[:, :TOP_K]
    slot = starts[e4] + rank[:, :TOP_K].astype(jnp.int32)
    slot_flat = slot.reshape(-1)
    n_blocks = (m * TOP_K) // MOE_TM + N_EXPERTS
    cap = n_blocks * MOE_TM
    blk0 = jnp.arange(n_blocks, dtype=jnp.int32) * MOE_TM
    block_e = jnp.minimum(jnp.searchsorted(pad_end, blk0, side='right'), N_EXPERTS - 1).astype(jnp.int32)
    is_new = jnp.concatenate([jnp.ones((1,), jnp.int32), (block_e[1:] != block_e[:-1]).astype(jnp.int32)])
    used = (blk0 < pad_end[-1]).astype(jnp.int32)
    meta = jnp.stack([block_e, is_new, used])
    xs = _moe_dispatch(hn, slot_flat, cap, MOE_TOK_TM)
    h = _moe_gu(xs, meta, w_gu, b_gu, layer, MOE_TM, MOE_GU_TN)
    yb = _moe_down(h, meta, w_down, b_down, layer, MOE_TM, MOE_DOWN_TN)
    return _moe_combine(yb, slot_flat, prob, x, mod, seg_tokens, 5, MOE_TOK_TM)


def _split_last(x, sizes):
    cuts = [int(s) for s in np.cumsum(sizes)[:-1]]
    return jnp.split(x, cuts, axis=-1)


def _relayout_w_in(w):
    z, xbc, dt, rw, q, k, v, gate = _split_last(
        w, (SSD_INNER, SSD_XBC, 2 * SSD_HEADS, RWKV_COLS, NA_INNER, NA_INNER, NA_INNER, N_BRANCHES * D_MODEL))
    padc = lambda a, n: jnp.pad(a, ((0, 0), (0, n - a.shape[1])))
    parts = [z, padc(rw, RWKV_PAD), xbc, padc(dt, DT_PAD), q, k, padc(v, COL_GATE - COL_V), gate]
    out = jnp.concatenate(parts, axis=1)
    return padc(out, IN_COLS_PAD).astype(BF16)


def kernel(x_prompt, x_sample, cache_na_k, cache_na_v, state_ssd, state_rwkv, c, c_ctx, norm1_g, norm2_g, w_mod, b_mod, w_in, ssd_conv_w, ssd_conv_b, ssd_dt_bias, ssd_a_log, ssd_d, ssd_norm_g, rwkv_mu, rwkv_w0, rwkv_w2, rwkv_a0, rwkv_a2, rwkv_g2, rwkv_kk, rwkv_ka, rwkv_rk, rwkv_ln_g, rwkv_ln_b, na_rpb, w_branch, w_out, router_w, router_b, moe_w_gu, moe_b_gu, moe_w_down, moe_b_down, final_g):
    nb, seq, d = x_prompt.shape
    db, dseq, _ = x_sample.shape
    n_ctx = nb * seq
    n_lat = db * dseq
    assert n_ctx == dseq, "modulation segments assume context tokens fill exactly one latent-length segment"
    seg_tokens = dseq
    n_seg = 1 + db
    m = n_ctx + n_lat
    rwkv_tm = 256
    tables = _scan_tables(nb, seq // RWKV_CHUNK, db, dseq // RWKV_CHUNK)
    ssd_tables = _scan_tables(nb, seq // SSD_CHUNK, db, dseq // SSD_CHUNK)

    x = jnp.concatenate([x_prompt.reshape(n_ctx, d), x_sample.reshape(n_lat, d)], axis=0)
    cond = jnp.zeros((SUBLANES, d), F32).at[0].set(c_ctx).at[1:1 + db].set(c)

    ctx_k, ctx_v, ctx_ssd, ctx_rwkv = [], [], [], []
    for l in range(DEPTH):
        mod = _modulation(cond, w_mod[l], b_mod[l])
        mod = mod.reshape(SUBLANES, 6, d)[:n_seg]
        mod = jnp.pad(mod, ((0, 0), (0, SUBLANES - 6), (0, 0)))
        p = _norm_matmul(x, norm1_g[l][None], mod, _relayout_w_in(w_in[l]), seg_tokens, 0, 1, 1024, IN_TN)

        lw = _rwkv_layer_weights(rwkv_mu[l], rwkv_w0[l], rwkv_w2[l], rwkv_a0[l], rwkv_a2[l], rwkv_g2[l],
                                 rwkv_kk[l], rwkv_ka[l], rwkv_rk[l], rwkv_tm)
        lw['ln_g'] = rwkv_ln_g[l][None]
        lw['ln_b'] = rwkv_ln_b[l][None]
        s0 = jnp.concatenate([jnp.zeros((nb,) + state_rwkv.shape[2:], F32), state_rwkv[:, l]], axis=0)
        y_rwkv, s_rwkv = rwkv_mix(p, lw, _pair_blockdiag(s0), tables, tm=rwkv_tm, n_ctx_tokens=n_ctx,
                                  ctx_len=seq, lat_len=dseq, pairs=8)
        ctx_rwkv.append(_pair_unblock(s_rwkv[:nb]))

        s0 = jnp.concatenate([jnp.zeros((nb,) + state_ssd.shape[2:], F32), state_ssd[:, l]], axis=0)
        y_ssd, s_ssd = ssd_mix(p, ssd_conv_w[l], ssd_conv_b[l], ssd_dt_bias[l], ssd_a_log[l], ssd_d[l],
                               ssd_norm_g[l], s0, ssd_tables, tm=256, n_ctx_tokens=n_ctx, ctx_len=seq, lat_len=dseq)
        ctx_ssd.append(s_ssd[:nb])

        hd = (NA_HEADS, HEAD_DIM)
        ctx_k.append(p[:n_ctx, COL_K:COL_K + NA_INNER].reshape((nb, seq) + hd))
        ctx_v.append(p[:n_ctx, COL_V:COL_V + NA_INNER].reshape((nb, seq) + hd))
        y_na_c = _context_attention(p, nb, seq)
        past = cache_na_k.shape[2]
        y_na_l = _neighbourhood_attention(p, cache_na_k[:, l].reshape(db, past, NA_INNER),
                                          cache_na_v[:, l].reshape(db, past, NA_INNER),
                                          _na_bias_table(na_rpb[l], dseq // GRID_W), n_ctx // dseq, db, dseq)
        y_na = jnp.concatenate([y_na_c, y_na_l], axis=0)

        merged = _branch_merge((y_ssd, y_rwkv, y_na), w_branch[l].astype(BF16), p, 512, 512)
        x = _proj_residual(merged, w_out[l].astype(BF16), x, mod, seg_tokens, 2, 1024, 512)

        x = _moe_layer(x, norm2_g[l][None], mod, router_w[l], router_b[l], moe_w_gu, moe_b_gu,
                       moe_w_down, moe_b_down, l, seg_tokens)

    y = _final_norm(x, final_g[None], 1024)
    y_prompt = y[:n_ctx].reshape(nb, seq, d)
    y_sample = y[n_ctx:].reshape(db, dseq, d)
    return (y_prompt, y_sample, jnp.stack(ctx_k, axis=1), jnp.stack(ctx_v, axis=1),
            jnp.stack(ctx_ssd, axis=1), jnp.stack(ctx_rwkv, axis=1))
```

```python
import functools
import math

import jax
import jax.numpy as jnp
import numpy as np
from jax import lax
from jax.experimental import pallas as pl
from jax.experimental.pallas import tpu as pltpu

F32 = jnp.float32
BF16 = jnp.bfloat16

D_MODEL = 2048
DEPTH = 4
GRID_W = 64
HEAD_DIM = 64
MIX_WIDTH = D_MODEL // 2
N_BRANCHES = 3
SSD_HEADS = MIX_WIDTH // HEAD_DIM
SSD_INNER = MIX_WIDTH
SSD_GROUPS = 2
SSD_STATE = 128
SSD_CHUNK = 128
SSD_CONV = 3
SSD_XBC = SSD_INNER + 2 * SSD_GROUPS * SSD_STATE
RWKV_HEADS = MIX_WIDTH // HEAD_DIM
RWKV_INNER = MIX_WIDTH
RWKV_DECAY_RANK = 64
RWKV_ICLR_RANK = 64
RWKV_GATE_RANK = 160
RWKV_COLS = 3 * RWKV_INNER + 2 * RWKV_DECAY_RANK + 2 * RWKV_ICLR_RANK + RWKV_GATE_RANK
RWKV_LN_EPS = 64e-5
NA_HEADS = MIX_WIDTH // HEAD_DIM
NA_INNER = MIX_WIDTH
WIN_R = 8
WIN_C = 16
N_EXPERTS = 32
TOP_K = 4
D_FF = D_MODEL
SWIGLU_LIMIT = 7.0
SWIGLU_ALPHA = 1.702
NORM_EPS = 1e-6

LANES = 128
SUBLANES = 8
VMEM_LIMIT = 56 * 1024 * 1024

RWKV_PAD = 3584
DT_PAD = LANES
COL_Z = 0
COL_RWKV = COL_Z + SSD_INNER
COL_XBC = COL_RWKV + RWKV_PAD
COL_DT = COL_XBC + SSD_XBC
COL_Q = COL_DT + DT_PAD
COL_K = COL_Q + NA_INNER
COL_V = COL_K + NA_INNER
IN_TN = 512
COL_QKV_END = COL_V + NA_INNER
COL_GATE = -(-COL_QKV_END // IN_TN) * IN_TN
IN_COLS_USED = COL_GATE + N_BRANCHES * D_MODEL
IN_COLS_PAD = -(-IN_COLS_USED // IN_TN) * IN_TN

RWKV_CHUNK = 64
DECAY_SCALE = math.exp(-0.5)


def _cparams(sem):
    return pltpu.CompilerParams(dimension_semantics=sem, vmem_limit_bytes=VMEM_LIMIT)


def _dot(a, b):
    return jnp.dot(a, b, preferred_element_type=F32)


def _dot_nt(a, b):
    return lax.dot_general(a, b, (((1,), (1,)), ((), ())), preferred_element_type=F32)


def _split3(x):
    hi = x.astype(BF16)
    r1 = x - hi.astype(F32)
    mid = r1.astype(BF16)
    lo = (r1 - mid.astype(F32)).astype(BF16)
    return hi, mid, lo


def _dot_const_lhs3(c_bf16, x):
    hi, mid, lo = _split3(x)
    return _dot(c_bf16, hi) + _dot(c_bf16, mid) + _dot(c_bf16, lo)


def _dot_const_rhs2(x, c_bf16):
    hi = x.astype(BF16)
    lo = (x - hi.astype(F32)).astype(BF16)
    return _dot(hi, c_bf16) + _dot(lo, c_bf16)


def _sigmoid(x):
    return 1.0 / (1.0 + jnp.exp(-x))


def _silu(x):
    return x * _sigmoid(x)


def _mod_kernel(c_ref, w_ref, b_ref, o_ref):
    a = _silu(c_ref[...]).astype(BF16)
    o_ref[...] = _dot(a, w_ref[...].astype(BF16)) + b_ref[...]


def _modulation(cond, w_mod, b_mod):
    n = w_mod.shape[1]
    tn = 1024
    return pl.pallas_call(
        _mod_kernel,
        grid=(n // tn,),
        in_specs=[pl.BlockSpec((SUBLANES, D_MODEL), lambda j: (0, 0)),
                  pl.BlockSpec((D_MODEL, tn), lambda j: (0, j)),
                  pl.BlockSpec((1, tn), lambda j: (0, j))],
        out_specs=pl.BlockSpec((SUBLANES, tn), lambda j: (0, j)),
        out_shape=jax.ShapeDtypeStruct((SUBLANES, n), F32),
        compiler_params=_cparams(("arbitrary",)),
        name="adaln_modulation",
    )(cond, w_mod, b_mod.reshape(1, n))


def _norm_matmul_kernel(x_ref, g_ref, mod_ref, w_ref, o_ref, h_ref, *, shift_row, scale_row):
    @pl.when(pl.program_id(1) == 0)
    def _():
        x = x_ref[...]
        y = x * lax.rsqrt(jnp.mean(x * x, axis=-1, keepdims=True) + NORM_EPS)
        y = y * g_ref[...]
        m = mod_ref[0]
        y = y * (1.0 + m[scale_row:scale_row + 1, :]) + m[shift_row:shift_row + 1, :]
        h_ref[...] = y.astype(BF16)

    o_ref[...] = _dot(h_ref[...], w_ref[...]).astype(o_ref.dtype)


def _norm_matmul(x, g, mod, w, seg_tokens, shift_row, scale_row, tm, tn, out_dtype=F32):
    m, d = x.shape
    n = w.shape[1]
    kern = functools.partial(_norm_matmul_kernel, shift_row=shift_row, scale_row=scale_row)
    return pl.pallas_call(
        kern,
        grid=(m // tm, n // tn),
        in_specs=[pl.BlockSpec((tm, d), lambda i, j: (i, 0)),
                  pl.BlockSpec((1, d), lambda i, j: (0, 0)),
                  pl.BlockSpec((1, SUBLANES, d), lambda i, j: ((i * tm) // seg_tokens, 0, 0)),
                  pl.BlockSpec((d, tn), lambda i, j: (0, j))],
        out_specs=pl.BlockSpec((tm, tn), lambda i, j: (i, j)),
        out_shape=jax.ShapeDtypeStruct((m, n), out_dtype),
        scratch_shapes=[pltpu.VMEM((tm, d), BF16)],
        compiler_params=_cparams(("arbitrary", "arbitrary")),
        name="norm_matmul",
    )(x, g, mod, w)


def _shifted(x, prev8, next8, mu_p, mu_n, has_prev, has_next):
    tm = x.shape[0]
    rows = lax.broadcasted_iota(jnp.int32, x.shape, 0)
    p_row = jnp.where(has_prev, prev8[SUBLANES - 1:SUBLANES, :], 0.0)
    n_row = jnp.where(has_next, next8[0:1, :], 0.0)
    x_prev = jnp.where(rows == 0, p_row, pltpu.roll(x, 1, axis=0))
    x_next = jnp.where(rows == tm - 1, n_row, pltpu.roll(x, tm - 1, axis=0))
    return x + mu_p * (x_prev - x) + mu_n * (x_next - x)


def _rwkv_prep_kernel(
        r_ref, r_p, r_n, k_ref, k_p, k_n, v_ref, v_p, v_n,
        wl_ref, wl_p, wl_n, al_ref, al_p, al_n, gl_ref, gl_p, gl_n,
        mu_r, mu_k, mu_v, mu_wl, mu_al, mu_gl,
        w0_ref, w2_ref, a0_ref, a2_ref, g2_ref, kkp_ref, kap_ref, rk_ref,
        tril_ref, triu_ref, ones_ref, seg_ref,
        alpha_o, beta_o, kappa_o, rho_o, kappa_p_o, beta_p_o, v_o, bonus_o, g_o, pc_o,
        *, tm, n_ctx_tokens, ctx_len, lat_len):
    i = pl.program_id(0)
    tok0 = i * tm
    in_ctx = tok0 < n_ctx_tokens
    pos = jnp.where(in_ctx, tok0 % ctx_len, (tok0 - n_ctx_tokens) % lat_len)
    seq_len = jnp.where(in_ctx, ctx_len, lat_len)
    has_prev = pos != 0
    has_next = pos + tm != seq_len

    def sh(ref, p, n, mu):
        return _shifted(ref[...], p[...], n[...], mu[0:1, :], mu[1:2, :], has_prev, has_next)

    r = sh(r_ref, r_p, r_n, mu_r)
    k = sh(k_ref, k_p, k_n, mu_k)
    v = sh(v_ref, v_p, v_n, mu_v)
    w_lo = sh(wl_ref, wl_p, wl_n, mu_wl)
    a_lo = sh(al_ref, al_p, al_n, mu_al)
    g_lo = sh(gl_ref, gl_p, gl_n, mu_gl)

    seg = seg_ref[...]
    kk = k * kkp_ref[...]
    ss = _dot_const_rhs2(kk * kk, seg)
    kk = kk / jnp.maximum(jnp.sqrt(ss), 1e-12)

    tw = jnp.tanh(w_lo).astype(BF16)
    ab = a_lo.astype(BF16)
    g_o[...] = _dot(_sigmoid(g_lo).astype(BF16), g2_ref[...])
    v_o[...] = v.astype(BF16)

    tri = (tril_ref[...], triu_ref[...])
    ones_blk = ones_ref[...]
    kap = kap_ref[...]
    k_sum = jnp.zeros_like(k)
    nch = tm // RWKV_CHUNK
    for d in range(2):
        logw = w0_ref[d] + _dot(tw, w2_ref[d])
        lw = -DECAY_SCALE * _sigmoid(logw)
        a = _sigmoid(a0_ref[d] + _dot(ab, a2_ref[d]))
        b = kk * a
        k_d = k * (1.0 + (a - 1.0) * kap)
        k_sum = k_sum + k_d
        cum = _dot_const_lhs3(tri[d], lw)
        tot = _dot_const_lhs3(ones_blk, lw)
        e_neg = jnp.exp(-cum)
        e_rest = jnp.exp(tot - cum)
        alpha_o[d] = (kk * jnp.exp(cum - lw)).astype(BF16)
        beta_o[d] = (b * e_neg).astype(BF16)
        kappa_o[d] = (k_d * e_neg).astype(BF16)
        rho_o[d] = (r * jnp.exp(cum)).astype(BF16)
        kappa_p_o[d] = (k_d * e_rest).astype(BF16)
        beta_p_o[d] = (b * e_rest).astype(BF16)
        pcs = jnp.exp(tot)
        for c in range(nch):
            pc_o[d, c] = pcs[c * RWKV_CHUNK:c * RWKV_CHUNK + 1, :]
    bonus_o[...] = _dot_const_rhs2(r * k_sum * rk_ref[...], seg) * v


def _rwkv_prep(p, lw, *, tm, n_ctx_tokens, ctx_len, lat_len):
    m = p.shape[0]
    nlb = RWKV_INNER // LANES
    hb = tm // SUBLANES
    last8 = m // SUBLANES - 1
    cb = COL_RWKV // LANES

    def tile(col_blk, width=LANES, per_j=True):
        wb = width // LANES
        if per_j:
            cm = lambda i, j: (i, col_blk + j)
            cp = lambda i, j: (jnp.maximum(i * hb - 1, 0), col_blk + j)
            cn = lambda i, j: (jnp.minimum((i + 1) * hb, last8), col_blk + j)
        else:
            cm = lambda i, j: (i, col_blk // wb)
            cp = lambda i, j: (jnp.maximum(i * hb - 1, 0), col_blk // wb)
            cn = lambda i, j: (jnp.minimum((i + 1) * hb, last8), col_blk // wb)
        return [pl.BlockSpec((tm, width), cm), pl.BlockSpec((SUBLANES, width), cp),
                pl.BlockSpec((SUBLANES, width), cn)]

    gw = 2 * LANES
    in_specs = (tile(cb) + tile(cb + nlb) + tile(cb + 2 * nlb)
                + tile(cb + 3 * nlb, per_j=False) + tile(cb + 3 * nlb + 1, per_j=False)
                + tile(cb + 3 * nlb + 2, width=gw, per_j=False))
    per_lane = lambda rows: pl.BlockSpec((rows, LANES), lambda i, j: (0, j))
    const2 = lambda a, b: pl.BlockSpec((a, b), lambda i, j: (0, 0))
    in_specs += [per_lane(2), per_lane(2), per_lane(2), const2(2, LANES), const2(2, LANES), const2(2, gw)]
    in_specs += [pl.BlockSpec((2, 1, LANES), lambda i, j: (0, 0, j)),
                 pl.BlockSpec((2, LANES, LANES), lambda i, j: (0, 0, j)),
                 pl.BlockSpec((2, 1, LANES), lambda i, j: (0, 0, j)),
                 pl.BlockSpec((2, LANES, LANES), lambda i, j: (0, 0, j)),
                 pl.BlockSpec((gw, LANES), lambda i, j: (0, j)),
                 per_lane(1), per_lane(1), per_lane(1),
                 const2(tm, tm), const2(tm, tm), const2(tm, tm), const2(LANES, LANES)]
    tok_d = lambda: pl.BlockSpec((2, tm, LANES), lambda i, j: (0, i, j))
    tok = lambda: pl.BlockSpec((tm, LANES), lambda i, j: (i, j))
    nch = tm // RWKV_CHUNK
    out_specs = [tok_d() for _ in range(6)] + [tok(), tok(), tok(),
                 pl.BlockSpec((2, nch, 1, LANES), lambda i, j: (0, i, 0, j))]
    sd = jax.ShapeDtypeStruct
    out_shape = [sd((2, m, RWKV_INNER), BF16) for _ in range(6)] + [
        sd((m, RWKV_INNER), BF16), sd((m, RWKV_INNER), F32), sd((m, RWKV_INNER), F32),
        sd((2, m // RWKV_CHUNK, 1, RWKV_INNER), F32)]
    kern = functools.partial(_rwkv_prep_kernel, tm=tm, n_ctx_tokens=n_ctx_tokens, ctx_len=ctx_len, lat_len=lat_len)
    args = [p] * 18 + [lw['mu_r'], lw['mu_k'], lw['mu_v'], lw['mu_wl'], lw['mu_al'], lw['mu_gl'],
                       lw['w0'], lw['w2'], lw['a0'], lw['a2'], lw['g2'], lw['kk'], lw['ka'], lw['rk'],
                       lw['tril'], lw['triu'], lw['ones'], lw['seg']]
    return pl.pallas_call(
        kern, grid=(m // tm, nlb), in_specs=in_specs, out_specs=out_specs, out_shape=out_shape,
        compiler_params=_cparams(("arbitrary", "arbitrary")), name="rwkv_prep",
    )(*args)


def _rwkv_scan_kernel(tab_ref,
                      al_f, be_f, ka_f, rh_f, kp_f, bp_f, v_f, pc_f,
                      al_b, be_b, ka_b, rh_b, kp_b, bp_b, v_b, pc_b,
                      s0_ref, of_ref, ob_ref, sfin_ref, s_scr, *, pairs):
    s = pl.program_id(1)
    is_first = tab_ref[2, s] == 1
    is_last = tab_ref[3, s] == 1

    @pl.when(is_first)
    def _():
        s_scr[...] = s0_ref[0]

    c2 = 2 * RWKV_CHUNK
    lane = lax.broadcasted_iota(jnp.int32, (1, LANES), 1)
    m0 = lane < HEAD_DIM
    ri = lax.broadcasted_iota(jnp.int32, (c2, c2), 0)
    ci = lax.broadcasted_iota(jnp.int32, (c2, c2), 1)
    eye = (ri == ci).astype(F32)
    same_head = (ri // HEAD_DIM) == (ci // HEAD_DIM)
    zero_b = jnp.zeros((), BF16)

    def stack(x):
        z = jnp.zeros_like(x)
        return jnp.concatenate([jnp.where(m0, x, z), jnp.where(m0, z, x)], axis=0)

    def fold(x):
        return x[:RWKV_CHUNK] + x[RWKV_CHUNK:]

    dirs = ((al_f, be_f, ka_f, rh_f, kp_f, bp_f, v_f, pc_f, of_ref, ci < ri, ci <= ri),
            (al_b, be_b, ka_b, rh_b, kp_b, bp_b, v_b, pc_b, ob_ref, ci > ri, ci >= ri))
    units = []
    for d, (al_r, be_r, ka_r, rh_r, kp_r, bp_r, v_r, pc_r, o_r, strict, incl) in enumerate(dirs):
        for g in range(pairs):
            sl = slice(g * LANES, (g + 1) * LANES)
            u = dict(d=d, g=g, sl=sl, o_r=o_r, strict=strict, incl=incl)
            u['al'], u['be'], u['ka'], u['rh'] = al_r[0, :, sl], be_r[0, :, sl], ka_r[0, :, sl], rh_r[0, :, sl]
            u['kp'], u['bp'], u['v'] = kp_r[0, :, sl], bp_r[0, :, sl], v_r[:, sl]
            u['pc'] = pc_r[0, 0, :, sl]
            u['s'] = s_scr[d, g]
            units.append(u)
    for u in units:
        lhs = jnp.concatenate([stack(u['al']), stack(u['rh'])], axis=0)
        u['sb'] = _dot_nt(lhs, stack(u['be']))
        u['sk'] = _dot_nt(lhs, stack(u['ka']))
    for u in units:
        l_k = jnp.where(u['strict'], u['sk'][:c2], 0.0)
        a_k = jnp.where(u['incl'], u['sk'][c2:], 0.0)
        u['a_b'] = fold(jnp.where(u['incl'], u['sb'][c2:], 0.0)).astype(BF16)
        u['lk_ak'] = jnp.concatenate([fold(l_k), fold(a_k)], axis=0).astype(BF16)
        u['x'] = -jnp.where(u['strict'], u['sb'][:c2], 0.0)
        u['t'] = eye + u['x']
    for u in units:
        u['m2'] = _dot_nt(jnp.concatenate([u['al'], u['rh']], axis=0), u['s'].astype(BF16))
        u['m3'] = _dot(u['lk_ak'], stack(u['v']))
    for _ in range(5):
        for u in units:
            xb = u['x'].astype(BF16)
            u['x'] = _dot(xb, xb)
        for u in units:
            u['t'] = u['t'] + _dot(u['t'].astype(BF16), u['x'].astype(BF16))
    for u in units:
        w1 = u['m2'][:RWKV_CHUNK] + u['m3'][:RWKV_CHUNK]
        u['u'] = fold(_dot(u['t'].astype(BF16), stack(w1.astype(BF16))))
    for u in units:
        ub = u['u'].astype(BF16)
        o = u['m2'][RWKV_CHUNK:] + u['m3'][RWKV_CHUNK:] - _dot(u['a_b'], stack(ub))
        u['o_r'][:, u['sl']] = o
        vu_t = jnp.concatenate([u['v'].astype(F32), u['u']], axis=0).T.astype(BF16)
        upd = _dot(vu_t, jnp.concatenate([u['kp'], -u['bp']], axis=0))
        s_scr[u['d'], u['g']] = u['s'] * u['pc'] + jnp.where(same_head, upd, 0.0)

    @pl.when(is_last)
    def _():
        sfin_ref[0] = s_scr[...]


def _rwkv_scan(ops, v, pc, s0, tables, *, pairs):
    m = v.shape[0]
    n_seq = s0.shape[0]
    steps = tables.shape[1]
    gl = pairs * LANES
    ng = RWKV_INNER // gl

    def dspec(d, row):
        return pl.BlockSpec((1, RWKV_CHUNK, gl), lambda g, s, tab: (d, tab[row, s], g))

    def dir_specs(d, row):
        return [dspec(d, row) for _ in range(6)] + [
            pl.BlockSpec((RWKV_CHUNK, gl), lambda g, s, tab: (tab[row, s], g)),
            pl.BlockSpec((1, 1, 1, gl), lambda g, s, tab: (d, tab[row, s], 0, g))]

    in_specs = dir_specs(0, 0) + dir_specs(1, 1) + [
        pl.BlockSpec((1, 2, pairs, LANES, LANES), lambda g, s, tab: (tab[4, s], 0, g, 0, 0))]
    out_specs = [pl.BlockSpec((RWKV_CHUNK, gl), lambda g, s, tab: (tab[0, s], g)),
                 pl.BlockSpec((RWKV_CHUNK, gl), lambda g, s, tab: (tab[1, s], g)),
                 pl.BlockSpec((1, 2, pairs, LANES, LANES), lambda g, s, tab: (tab[4, s], 0, g, 0, 0))]
    sd = jax.ShapeDtypeStruct
    out_shape = [sd((m, RWKV_INNER), F32), sd((m, RWKV_INNER), F32),
                 sd((n_seq, 2, RWKV_HEADS // 2, LANES, LANES), F32)]
    grid_spec = pltpu.PrefetchScalarGridSpec(
        num_scalar_prefetch=1, grid=(ng, steps), in_specs=in_specs, out_specs=out_specs,
        scratch_shapes=[pltpu.VMEM((2, pairs, LANES, LANES), F32)])
    args = list(ops) + [v, pc] + list(ops) + [v, pc] + [s0]
    return pl.pallas_call(
        functools.partial(_rwkv_scan_kernel, pairs=pairs), grid_spec=grid_spec, out_shape=out_shape,
        compiler_params=_cparams(("arbitrary", "arbitrary")), name="rwkv_scan",
    )(tables, *args)


def _rwkv_post_kernel(of_ref, ob_ref, bonus_ref, g_ref, lng_ref, lnb_ref, seg_ref, y_ref):
    o = of_ref[...] + ob_ref[...]
    seg = seg_ref[...]
    inv = 1.0 / HEAD_DIM
    nlb = o.shape[1] // LANES
    for j in range(nlb):
        sl = slice(j * LANES, (j + 1) * LANES)
        oj = o[:, sl]
        mu = _dot_const_rhs2(oj, seg) * inv
        cj = oj - mu
        var = _dot_const_rhs2(cj * cj, seg) * inv
        yj = cj * lax.rsqrt(var + RWKV_LN_EPS) * lng_ref[:, sl] + lnb_ref[:, sl]
        y_ref[:, sl] = ((yj + bonus_ref[:, sl]) * g_ref[:, sl]).astype(y_ref.dtype)


def _rwkv_post(o_f, o_b, bonus, g, ln_g, ln_b, seg, tm):
    m, w = o_f.shape
    tok = lambda: pl.BlockSpec((tm, w), lambda i: (i, 0))
    row = lambda: pl.BlockSpec((1, w), lambda i: (0, 0))
    return pl.pallas_call(
        _rwkv_post_kernel, grid=(m // tm,),
        in_specs=[tok(), tok(), tok(), tok(), row(), row(), pl.BlockSpec((LANES, LANES), lambda i: (0, 0))],
        out_specs=tok(), out_shape=jax.ShapeDtypeStruct((m, w), BF16),
        compiler_params=_cparams(("arbitrary",)), name="rwkv_post",
    )(o_f, o_b, bonus, g, ln_g, ln_b, seg)


def _seg_ones():
    i = np.arange(LANES)
    return jnp.asarray((i[:, None] // HEAD_DIM) == (i[None, :] // HEAD_DIM), BF16)


def _rwkv_consts(tm):
    i = np.arange(tm)
    same = (i[:, None] // RWKV_CHUNK) == (i[None, :] // RWKV_CHUNK)
    tril = same & (i[None, :] <= i[:, None])
    triu = same & (i[None, :] >= i[:, None])
    return jnp.asarray(tril, BF16), jnp.asarray(triu, BF16), jnp.asarray(same, BF16)


def _rwkv_layer_weights(rwkv_mu, w0, w2, a0, a2, g2, kkp, kap, rk, tm):
    inner = RWKV_INNER
    mu = rwkv_mu
    o_wl, o_al, o_gl = 3 * inner, 3 * inner + 128, 3 * inner + 256
    pad_g = 2 * LANES - RWKV_GATE_RANK

    def bd(w):
        z = jnp.zeros_like(w[0])
        return jnp.stack([jnp.concatenate([w[0], z], 0), jnp.concatenate([z, w[1]], 0)]).astype(BF16)

    tril, triu, ones = _rwkv_consts(tm)
    return {
        'mu_r': mu[:, 0:inner], 'mu_k': mu[:, inner:2 * inner], 'mu_v': mu[:, 2 * inner:3 * inner],
        'mu_wl': mu[:, o_wl:o_wl + 128], 'mu_al': mu[:, o_al:o_al + 128],
        'mu_gl': jnp.pad(mu[:, o_gl:], ((0, 0), (0, pad_g))),
        'w0': w0[:, None, :], 'w2': bd(w2), 'a0': a0[:, None, :], 'a2': bd(a2),
        'g2': jnp.pad(g2, ((0, pad_g), (0, 0))).astype(BF16),
        'kk': kkp[None, :], 'ka': kap[None, :], 'rk': rk.reshape(1, inner),
        'tril': tril, 'triu': triu, 'ones': ones, 'seg': _seg_ones(),
    }


def _scan_tables(n_ctx_seq, ctx_chunks, n_lat_seq, lat_chunks):
    fwd, bwd, first, last, seq = [], [], [], [], []
    base = 0
    sid = 0
    for n_seq, n in ((n_ctx_seq, ctx_chunks), (n_lat_seq, lat_chunks)):
        for q in range(n_seq):
            for c in range(n):
                fwd.append(base + c)
                bwd.append(base + n - 1 - c)
                first.append(int(c == 0))
                last.append(int(c == n - 1))
                seq.append(sid)
            base += n
            sid += 1
    return jnp.asarray(np.array([fwd, bwd, first, last, seq], np.int32))


def _pair_blockdiag(s):
    sh = s.shape[:-3]
    h = s.shape[-3]
    s = s.reshape(sh + (h // 2, 2, HEAD_DIM, HEAD_DIM))
    z = jnp.zeros_like(s[..., 0, :, :])
    top = jnp.concatenate([s[..., 0, :, :], z], axis=-1)
    bot = jnp.concatenate([z, s[..., 1, :, :]], axis=-1)
    return jnp.concatenate([top, bot], axis=-2)


def _pair_unblock(s):
    a = s[..., :HEAD_DIM, :HEAD_DIM]
    b = s[..., HEAD_DIM:, HEAD_DIM:]
    out = jnp.stack([a, b], axis=-3)
    return out.reshape(s.shape[:-3] + (2 * s.shape[-3], HEAD_DIM, HEAD_DIM))


def rwkv_mix(p, lw, s0_bd, tables, *, tm, n_ctx_tokens, ctx_len, lat_len, pairs):
    outs = _rwkv_prep(p, lw, tm=tm, n_ctx_tokens=n_ctx_tokens, ctx_len=ctx_len, lat_len=lat_len)
    ops, (v, bonus, g, pc) = outs[:6], outs[6:]
    o_f, o_b, s_fin = _rwkv_scan(ops, v, pc, s0_bd, tables, pairs=pairs)
    y = _rwkv_post(o_f, o_b, bonus, g, lw['ln_g'], lw['ln_b'], lw['seg'], tm)
    return y, s_fin


SSD_BC = SSD_GROUPS * SSD_STATE


def _softplus(x):
    return jnp.maximum(x, 0.0) + jnp.log(1.0 + jnp.exp(-jnp.abs(x)))


def _ssd_prep_kernel(x_ref, x_p, x_n, dt_ref, w_ref, b_ref, dtb_ref, aneg_ref, act_o, dt_o, a_o,
                     *, tm, n_ctx_tokens, ctx_len, lat_len):
    i = pl.program_id(0)
    tok0 = i * tm
    in_ctx = tok0 < n_ctx_tokens
    pos = jnp.where(in_ctx, tok0 % ctx_len, (tok0 - n_ctx_tokens) % lat_len)
    seq_len = jnp.where(in_ctx, ctx_len, lat_len)
    has_prev = pos != 0
    has_next = pos + tm != seq_len
    x = x_ref[...]
    rows = lax.broadcasted_iota(jnp.int32, x.shape, 0)
    p_row = jnp.where(has_prev, x_p[SUBLANES - 1:SUBLANES, :], 0.0)
    n_row = jnp.where(has_next, x_n[0:1, :], 0.0)
    x_prev = jnp.where(rows == 0, p_row, pltpu.roll(x, 1, axis=0))
    x_next = jnp.where(rows == tm - 1, n_row, pltpu.roll(x, tm - 1, axis=0))
    y = x_prev * w_ref[0:1, :] + x * w_ref[1:2, :] + x_next * w_ref[2:3, :] + b_ref[...]
    act_o[...] = _silu(y).astype(act_o.dtype)
    dt = _softplus(dt_ref[...] + dtb_ref[...])
    dt_o[...] = dt
    a_o[...] = dt * aneg_ref[...]


def _ssd_prep(p, conv_w, conv_b, dt_bias, a_neg, *, tm, n_ctx_tokens, ctx_len, lat_len):
    m = p.shape[0]
    hb = tm // SUBLANES
    last8 = m // SUBLANES - 1
    wx = SSD_XBC
    cb = COL_XBC // wx if COL_XBC % wx == 0 else None
    assert cb is not None
    pad = lambda v: jnp.pad(v.reshape(1, -1), ((0, 0), (0, DT_PAD - v.size)))
    kern = functools.partial(_ssd_prep_kernel, tm=tm, n_ctx_tokens=n_ctx_tokens, ctx_len=ctx_len, lat_len=lat_len)
    sd = jax.ShapeDtypeStruct
    return pl.pallas_call(
        kern, grid=(m // tm,),
        in_specs=[pl.BlockSpec((tm, wx), lambda i: (i, cb)),
                  pl.BlockSpec((SUBLANES, wx), lambda i: (jnp.maximum(i * hb - 1, 0), cb)),
                  pl.BlockSpec((SUBLANES, wx), lambda i: (jnp.minimum((i + 1) * hb, last8), cb)),
                  pl.BlockSpec((tm, DT_PAD), lambda i: (i, COL_DT // DT_PAD)),
                  pl.BlockSpec((SSD_CONV, wx), lambda i: (0, 0)), pl.BlockSpec((1, wx), lambda i: (0, 0)),
                  pl.BlockSpec((1, DT_PAD), lambda i: (0, 0)), pl.BlockSpec((1, DT_PAD), lambda i: (0, 0))],
        out_specs=[pl.BlockSpec((tm, wx), lambda i: (i, 0)), pl.BlockSpec((tm, DT_PAD), lambda i: (i, 0)),
                   pl.BlockSpec((tm, DT_PAD), lambda i: (i, 0))],
        out_shape=[sd((m, wx), BF16), sd((m, DT_PAD), F32), sd((m, DT_PAD), F32)],
        compiler_params=_cparams(("arbitrary",)), name="ssd_prep",
    )(p, p, p, p, conv_w, conv_b.reshape(1, wx), pad(dt_bias), pad(a_neg))


def _ssd_scan_kernel(tab_ref, x_f, b_f, c_f, dt_f, a_f, x_b, b_b, c_b, dt_b, a_b, tril_ref, triu_ref,
                     s0_ref, yf_ref, yb_ref, sfin_ref, s_scr):
    s = pl.program_id(0)

    @pl.when(tab_ref[2, s] == 1)
    def _():
        s_scr[...] = s0_ref[0]

    c = SSD_CHUNK
    lane0 = lax.broadcasted_iota(jnp.int32, (1, LANES), 1) < HEAD_DIM
    ri = lax.broadcasted_iota(jnp.int32, (c, c), 0)
    ci = lax.broadcasted_iota(jnp.int32, (c, c), 1)
    npair = SSD_HEADS // 2
    hpg = SSD_HEADS // SSD_GROUPS
    dirs = ((x_f, b_f, c_f, dt_f, a_f, tril_ref, yf_ref, ci <= ri),
            (x_b, b_b, c_b, dt_b, a_b, triu_ref, yb_ref, ci >= ri))
    for d, (x_r, b_r, c_r, dt_r, a_r, tri_r, y_r, causal) in enumerate(dirs):
        cum = _dot_const_lhs3(tri_r[...], a_r[...])
        tot = cum[c - 1:c, :] if d == 0 else cum[0:1, :]
        cum_t = cum.T
        dt_t = dt_r[...].T
        gs = []
        for g in range(SSD_GROUPS):
            gl = slice(g * SSD_STATE, (g + 1) * SSD_STATE)
            gs.append(_dot_nt(c_r[:, gl], b_r[:, gl]))
        for pr in range(npair):
            sl = slice(pr * LANES, (pr + 1) * LANES)
            grp = (2 * pr) // hpg
            gl = slice(grp * SSD_STATE, (grp + 1) * SSD_STATE)
            xp = x_r[:, sl]
            ms, e_in, wgt, dec = [], [], [], []
            for hh in range(2):
                col = d * SSD_HEADS + 2 * pr + hh
                cc = cum[:, col:col + 1]
                diff = jnp.where(causal, cc - cum_t[col:col + 1, :], MASK_BIAS)
                ms.append((gs[grp] * jnp.exp(diff) * dt_t[col:col + 1, :]).astype(BF16))
                e_in.append(jnp.exp(cc))
                wgt.append(jnp.exp(tot[:, col:col + 1] - cc) * dt_r[:, col:col + 1])
                dec.append(jnp.exp(tot[:, col:col + 1]))
            z = jnp.zeros_like(xp)
            x_stack = jnp.concatenate([jnp.where(lane0, xp, z), jnp.where(lane0, z, xp)], axis=0)
            y_diag = _dot(jnp.concatenate(ms, axis=1), x_stack)
            s_pair = s_scr[d, pr]
            y_off = _dot_nt(c_r[:, gl], s_pair.astype(BF16)) * jnp.where(lane0, e_in[0], e_in[1])
            y_r[:, sl] = y_diag + y_off
            xw = xp.astype(F32) * jnp.where(lane0, wgt[0], wgt[1])
            upd = _dot(xw.T.astype(BF16), b_r[:, gl])
            rows_h0 = lax.broadcasted_iota(jnp.int32, (LANES, 1), 0) < HEAD_DIM
            s_scr[d, pr] = s_pair * jnp.where(rows_h0, dec[0], dec[1]) + upd

    @pl.when(tab_ref[3, s] == 1)
    def _():
        sfin_ref[0] = s_scr[...]


def _ssd_scan(act, dt, a, s0, tables):
    m = act.shape[0]
    n_seq = s0.shape[0]
    steps = tables.shape[1]
    c = SSD_CHUNK
    i = np.arange(c)
    tril = jnp.asarray(i[None, :] <= i[:, None], BF16)
    triu = jnp.asarray(i[None, :] >= i[:, None], BF16)
    nxb = SSD_INNER // SSD_BC

    def dir_specs(row):
        return [pl.BlockSpec((c, SSD_INNER), lambda s, tab: (tab[row, s], 0)),
                pl.BlockSpec((c, SSD_BC), lambda s, tab: (tab[row, s], nxb)),
                pl.BlockSpec((c, SSD_BC), lambda s, tab: (tab[row, s], nxb + 1)),
                pl.BlockSpec((c, DT_PAD), lambda s, tab: (tab[row, s], 0)),
                pl.BlockSpec((c, DT_PAD), lambda s, tab: (tab[row, s], 0))]

    st_spec = lambda: pl.BlockSpec((1, 2, SSD_HEADS // 2, LANES, SSD_STATE), lambda s, tab: (tab[4, s], 0, 0, 0, 0))
    in_specs = dir_specs(0) + dir_specs(1) + [pl.BlockSpec((c, c), lambda s, tab: (0, 0)),
                                              pl.BlockSpec((c, c), lambda s, tab: (0, 0)), st_spec()]
    out_specs = [pl.BlockSpec((c, SSD_INNER), lambda s, tab: (tab[0, s], 0)),
                 pl.BlockSpec((c, SSD_INNER), lambda s, tab: (tab[1, s], 0)), st_spec()]
    sd = jax.ShapeDtypeStruct
    out_shape = [sd((m, SSD_INNER), F32), sd((m, SSD_INNER), F32), sd(s0.shape, F32)]
    grid_spec = pltpu.PrefetchScalarGridSpec(
        num_scalar_prefetch=1, grid=(steps,), in_specs=in_specs, out_specs=out_specs,
        scratch_shapes=[pltpu.VMEM(s0.shape[1:], F32)])
    args = [act, act, act, dt, a] * 2 + [tril, triu, s0]
    return pl.pallas_call(
        _ssd_scan_kernel, grid_spec=grid_spec, out_shape=out_shape,
        compiler_params=_cparams(("arbitrary",)), name="ssd_scan",
    )(tables, *args)


def _ssd_post_kernel(yf_ref, yb_ref, x_ref, z_ref, d_ref, g_ref, o_ref):
    y = (x_ref[...].astype(F32) * d_ref[...] + yf_ref[...] + yb_ref[...]) * _silu(z_ref[...])
    gw = SSD_INNER // SSD_GROUPS
    for g in range(SSD_GROUPS):
        sl = slice(g * gw, (g + 1) * gw)
        yg = y[:, sl]
        yg = yg * lax.rsqrt(jnp.mean(yg * yg, axis=-1, keepdims=True) + NORM_EPS)
        o_ref[:, sl] = (yg * g_ref[:, sl]).astype(o_ref.dtype)


def _ssd_post(y_f, y_b, act, p, d_lanes, norm_g, tm):
    m = y_f.shape[0]
    w = SSD_INNER
    tok = lambda: pl.BlockSpec((tm, w), lambda i: (i, 0))
    row = lambda: pl.BlockSpec((1, w), lambda i: (0, 0))
    return pl.pallas_call(
        _ssd_post_kernel, grid=(m // tm,),
        in_specs=[tok(), tok(), tok(), pl.BlockSpec((tm, w), lambda i: (i, COL_Z // w)), row(), row()],
        out_specs=tok(), out_shape=jax.ShapeDtypeStruct((m, w), BF16),
        compiler_params=_cparams(("arbitrary",)), name="ssd_post",
    )(y_f, y_b, act, p, d_lanes, norm_g)


def ssd_mix(p, conv_w, conv_b, dt_bias, a_log, d_skip, norm_g, s0, tables, *, tm, n_ctx_tokens, ctx_len, lat_len):
    a_neg = -jnp.exp(a_log)
    act, dt, a = _ssd_prep(p, conv_w, conv_b, dt_bias, a_neg, tm=tm, n_ctx_tokens=n_ctx_tokens,
                           ctx_len=ctx_len, lat_len=lat_len)
    sh = s0.shape
    s0p = s0.reshape(sh[0], 2, SSD_HEADS // 2, LANES, SSD_STATE)
    y_f, y_b, s_fin = _ssd_scan(act, dt, a, s0p, tables)
    d_lanes = jnp.repeat(d_skip, HEAD_DIM)[None]
    y = _ssd_post(y_f, y_b, act, p, d_lanes, norm_g[None], tm)
    return y, s_fin.reshape(sh)


ATTN_SCALE = HEAD_DIM ** -0.5
MASK_BIAS = -1e30
NA_Q_ROWS = 4
NA_K_ROWS = 12


def _softmax_pv(q_heads, key_sets, lane_head0):
    out = None
    for h, qm in enumerate(q_heads):
        scores = []
        for kb, _, bias in key_sets:
            s = _dot_nt(qm, kb)
            scores.append(s if bias is None else s + bias[h])
        mx = scores[0].max(axis=-1, keepdims=True)
        for s in scores[1:]:
            mx = jnp.maximum(mx, s.max(axis=-1, keepdims=True))
        den = 0.0
        acc = 0.0
        for s, (_, vb, _) in zip(scores, key_sets):
            e = jnp.exp(s - mx)
            den = den + e.sum(axis=-1, keepdims=True)
            acc = acc + _dot(e.astype(BF16), vb)
        o = acc / den
        keep = lane_head0 if h == 0 else jnp.logical_not(lane_head0)
        o = jnp.where(keep, o, 0.0)
        out = o if out is None else out + o
    return out


def _head_queries(q):
    lane0 = lax.broadcasted_iota(jnp.int32, (1, LANES), 1) < HEAD_DIM
    qs = q * ATTN_SCALE
    return [jnp.where(lane0, qs, 0.0).astype(BF16), jnp.where(lane0, 0.0, qs).astype(BF16)], lane0


def _ctx_attn_kernel(q_ref, k_ref, v_ref, o_ref):
    q_heads, lane0 = _head_queries(q_ref[...])
    sets = [(k_ref[...].astype(BF16), v_ref[...].astype(BF16), None)]
    o_ref[...] = _softmax_pv(q_heads, sets, lane0).astype(o_ref.dtype)


def _context_attention(p, n_seq, seq):
    npair = NA_INNER // LANES
    spec = lambda col: pl.BlockSpec((seq, LANES), lambda b, g: (b, col // LANES + g))
    return pl.pallas_call(
        _ctx_attn_kernel, grid=(n_seq, npair),
        in_specs=[spec(COL_Q), spec(COL_K), spec(COL_V)],
        out_specs=pl.BlockSpec((seq, LANES), lambda b, g: (b, g)),
        out_shape=jax.ShapeDtypeStruct((n_seq * seq, NA_INNER), BF16),
        compiler_params=_cparams(("arbitrary", "arbitrary")), name="context_attention",
    )(p, p, p)


def _na_geometry(rows):
    wr = min(WIN_R, rows)
    assert wr == WIN_R and rows % NA_Q_ROWS == 0 and rows >= NA_K_ROWS
    n_rb = rows // NA_Q_ROWS
    r_ids = np.arange(rows)
    key_r0 = np.clip(r_ids - wr // 2, 0, rows - wr)
    starts = np.clip(np.arange(n_rb) * NA_Q_ROWS - wr // 2, 0, rows - NA_K_ROWS)
    sigs, types = [], []
    for rb in range(n_rb):
        rr = np.arange(rb * NA_Q_ROWS, (rb + 1) * NA_Q_ROWS)
        assert starts[rb] <= key_r0[rr].min() and key_r0[rr].max() + wr <= starts[rb] + NA_K_ROWS
        sig = (int(starts[rb] - rb * NA_Q_ROWS),) + tuple(int(x) for x in key_r0[rr] - rb * NA_Q_ROWS)
        if sig not in sigs:
            sigs.append(sig)
        types.append(sigs.index(sig))
    return n_rb, starts.astype(np.int32), np.array(types, np.int32), sigs


def _na_bias_table(rpb, rows):
    _, _, _, sigs = _na_geometry(rows)
    h = rpb.shape[0]
    c_ids = np.arange(GRID_W)
    key_c0 = np.clip(c_ids - WIN_C // 2, 0, GRID_W - WIN_C)
    col_ok = (c_ids[None, :] >= key_c0[:, None]) & (c_ids[None, :] < key_c0[:, None] + WIN_C)
    pad = GRID_W - WIN_C
    rp = jnp.pad(rpb, ((0, 0), (0, 0), (pad, pad)))
    toep = jnp.stack([rp[:, :, GRID_W - 1 - c:2 * GRID_W - 1 - c] for c in range(GRID_W)], axis=2)
    toep = jnp.where(jnp.asarray(col_ok)[None, None], toep, MASK_BIAS)
    masked = jnp.full((h, GRID_W, GRID_W), MASK_BIAS, F32)
    tabs = []
    for sig in sigs:
        start_rel, kr0_rel = sig[0], sig[1:]
        q_rows = []
        for rr in range(NA_Q_ROWS):
            blocks = []
            for kk in range(NA_K_ROWS):
                kr = start_rel + kk
                inside = kr0_rel[rr] <= kr < kr0_rel[rr] + WIN_R
                blocks.append(toep[:, kr - rr + WIN_R - 1] if inside else masked)
            q_rows.append(jnp.concatenate(blocks, axis=-1))
        tabs.append(jnp.concatenate(q_rows, axis=-2))
    return jnp.stack(tabs, axis=1)


def _nbr_attn_kernel(tab_ref, q_ref, k_ref, v_ref, kc_ref, vc_ref, bias_ref, o_ref):
    rb = pl.program_id(2)
    start = pl.multiple_of(tab_ref[0, rb] * GRID_W, GRID_W)
    n_k = NA_K_ROWS * GRID_W
    q_heads, lane0 = _head_queries(q_ref[...])
    sets = [(k_ref[pl.ds(start, n_k), :].astype(BF16), v_ref[pl.ds(start, n_k), :].astype(BF16), bias_ref),
            (kc_ref[0].astype(BF16), vc_ref[0].astype(BF16), None)]
    o_ref[...] = _softmax_pv(q_heads, sets, lane0).astype(o_ref.dtype)


def _neighbourhood_attention(p, k_ctx, v_ctx, bias_tab, row_off_blocks, n_b, t):
    rows = t // GRID_W
    n_rb, starts, types, _ = _na_geometry(rows)
    tq = NA_Q_ROWS * GRID_W
    npair = NA_INNER // LANES
    past = k_ctx.shape[1]
    tabs = jnp.asarray(np.stack([starts, types]))
    qspec = pl.BlockSpec((tq, LANES), lambda b, g, r, tab: ((row_off_blocks + b) * n_rb + r, COL_Q // LANES + g))
    kv = lambda col: pl.BlockSpec((t, LANES), lambda b, g, r, tab: (row_off_blocks + b, col // LANES + g))
    cspec = lambda: pl.BlockSpec((1, past, LANES), lambda b, g, r, tab: (b, 0, g))
    bspec = pl.BlockSpec((2, None, tq, NA_K_ROWS * GRID_W), lambda b, g, r, tab: (g, tab[1, r], 0, 0))
    grid_spec = pltpu.PrefetchScalarGridSpec(
        num_scalar_prefetch=1, grid=(n_b, npair, n_rb),
        in_specs=[qspec, kv(COL_K), kv(COL_V), cspec(), cspec(), bspec],
        out_specs=pl.BlockSpec((tq, LANES), lambda b, g, r, tab: (b * n_rb + r, g)))
    return pl.pallas_call(
        _nbr_attn_kernel, grid_spec=grid_spec,
        out_shape=jax.ShapeDtypeStruct((n_b * t, NA_INNER), BF16),
        compiler_params=_cparams(("arbitrary", "arbitrary", "arbitrary")), name="neighbourhood_attention",
    )(tabs, p, p, p, k_ctx, v_ctx, bias_tab)


def _branch_merge_kernel(y0_ref, y1_ref, y2_ref, w_ref, g0_ref, g1_ref, g2_ref, o_ref):
    acc = _sigmoid(g0_ref[...]) * _dot(y0_ref[...], w_ref[0])
    acc += _sigmoid(g1_ref[...]) * _dot(y1_ref[...], w_ref[1])
    acc += _sigmoid(g2_ref[...]) * _dot(y2_ref[...], w_ref[2])
    o_ref[...] = acc.astype(o_ref.dtype)


def _branch_merge(ys, w_branch, p, tm, tn):
    m, kdim = ys[0].shape
    n = w_branch.shape[2]
    gb = COL_GATE // tn
    yspec = lambda: pl.BlockSpec((tm, kdim), lambda i, j: (i, 0))
    gspec = lambda b: pl.BlockSpec((tm, tn), lambda i, j: (i, gb + b * (n // tn) + j))
    return pl.pallas_call(
        _branch_merge_kernel, grid=(m // tm, n // tn),
        in_specs=[yspec(), yspec(), yspec(), pl.BlockSpec((N_BRANCHES, kdim, tn), lambda i, j: (0, 0, j)),
                  gspec(0), gspec(1), gspec(2)],
        out_specs=pl.BlockSpec((tm, tn), lambda i, j: (i, j)),
        out_shape=jax.ShapeDtypeStruct((m, n), BF16),
        compiler_params=_cparams(("arbitrary", "arbitrary")), name="branch_merge",
    )(ys[0], ys[1], ys[2], w_branch, p, p, p)


def _proj_residual_kernel(a_ref, w_ref, x_ref, mod_ref, o_ref, *, gate_row):
    g = mod_ref[0][gate_row:gate_row + 1, :]
    o_ref[...] = x_ref[...] + g * _dot(a_ref[...], w_ref[...])


def _proj_residual(a, w, x, mod, seg_tokens, gate_row, tm, tn):
    m, kdim = a.shape
    n = w.shape[1]
    return pl.pallas_call(
        functools.partial(_proj_residual_kernel, gate_row=gate_row), grid=(m // tm, n // tn),
        in_specs=[pl.BlockSpec((tm, kdim), lambda i, j: (i, 0)),
                  pl.BlockSpec((kdim, tn), lambda i, j: (0, j)),
                  pl.BlockSpec((tm, tn), lambda i, j: (i, j)),
                  pl.BlockSpec((1, SUBLANES, tn), lambda i, j: ((i * tm) // seg_tokens, 0, j))],
        out_specs=pl.BlockSpec((tm, tn), lambda i, j: (i, j)),
        out_shape=jax.ShapeDtypeStruct((m, n), F32),
        compiler_params=_cparams(("arbitrary", "arbitrary")), name="proj_residual",
    )(a, w, x, mod)


def _final_norm_kernel(x_ref, g_ref, o_ref):
    x = x_ref[...]
    o_ref[...] = x * lax.rsqrt(jnp.mean(x * x, axis=-1, keepdims=True) + NORM_EPS) * g_ref[...]


def _final_norm(x, g, tm):
    m, d = x.shape
    return pl.pallas_call(
        _final_norm_kernel, grid=(m // tm,),
        in_specs=[pl.BlockSpec((tm, d), lambda i: (i, 0)), pl.BlockSpec((1, d), lambda i: (0, 0))],
        out_specs=pl.BlockSpec((tm, d), lambda i: (i, 0)),
        out_shape=jax.ShapeDtypeStruct((m, d), F32),
        compiler_params=_cparams(("arbitrary",)), name="final_norm",
    )(x, g)


MOE_TM = 512
MOE_TOK_TM = 256
MOE_GU_TN = 1024
MOE_DOWN_TN = 1024
NEG_BIG = -3.0e38


def _moe_route_kernel(x_ref, g_ref, mod_ref, wh_ref, wl_ref, b_ref, tril_ref,
                      hn_ref, idx_ref, rank_ref, prob_ref, cnt_ref, carry, *, shift_row, scale_row):
    @pl.when(pl.program_id(0) == 0)
    def _():
        carry[...] = jnp.zeros_like(carry)

    x = x_ref[...]
    y = x * lax.rsqrt(jnp.mean(x * x, axis=-1, keepdims=True) + NORM_EPS) * g_ref[...]
    m = mod_ref[0]
    hn = y * (1.0 + m[scale_row:scale_row + 1, :]) + m[shift_row:shift_row + 1, :]
    hn_ref[...] = hn
    hi = hn.astype(BF16)
    lo = (hn - hi.astype(F32)).astype(BF16)
    wh = wh_ref[...]
    logits = _dot(hi, wh) + _dot(lo, wh) + _dot(hi, wl_ref[...]) + b_ref[...]

    lane_i = lax.broadcasted_iota(jnp.int32, logits.shape, 1)
    lane = lane_i.astype(F32)
    work = logits
    sel = jnp.zeros(logits.shape, jnp.bool_)
    picks = []
    m0 = None
    for k in range(TOP_K):
        mx = work.max(axis=-1, keepdims=True)
        idx = jnp.min(jnp.where(work == mx, lane, float(LANES)), axis=-1, keepdims=True)
        onehot = lane == idx
        sel = jnp.logical_or(sel, onehot)
        work = jnp.where(onehot, NEG_BIG, work)
        picks.append((idx, onehot))
        if k == 0:
            m0 = mx
    e = jnp.where(sel, jnp.exp(logits - m0), 0.0)
    probs = e / e.sum(axis=-1, keepdims=True)

    self_f = jnp.where(sel, 1.0, 0.0)
    before = _dot(tril_ref[...], self_f.astype(BF16)) + carry[...]
    carry[...] = before[-1:, :] + self_f[-1:, :]
    cnt_ref[...] = jnp.broadcast_to(carry[...], cnt_ref.shape)

    idx_o = jnp.zeros(logits.shape, jnp.int32)
    rank_o = jnp.zeros(logits.shape, F32)
    prob_o = jnp.zeros(logits.shape, F32)
    for k, (idx, onehot) in enumerate(picks):
        at_k = lane_i == k
        idx_o = jnp.where(at_k, idx.astype(jnp.int32), idx_o)
        rank_o = jnp.where(at_k, jnp.where(onehot, before, 0.0).sum(axis=-1, keepdims=True), rank_o)
        prob_o = jnp.where(at_k, jnp.where(onehot, probs, 0.0).sum(axis=-1, keepdims=True), prob_o)
    idx_ref[...] = idx_o
    rank_ref[...] = rank_o
    prob_ref[...] = prob_o


def _moe_route(x, g, mod, router_w, router_b, seg_tokens, shift_row, scale_row, tm):
    m, d = x.shape
    ne = router_w.shape[1]
    w = jnp.pad(router_w, ((0, 0), (0, LANES - ne)))
    wh = w.astype(BF16)
    wl = (w - wh.astype(F32)).astype(BF16)
    b = jnp.concatenate([router_b, jnp.full((LANES - ne,), MASK_BIAS, F32)])[None]
    i = np.arange(tm)
    tril = jnp.asarray(i[None, :] < i[:, None], BF16)
    row = lambda: pl.BlockSpec((tm, LANES), lambda t: (t, 0))
    sd = jax.ShapeDtypeStruct
    kern = functools.partial(_moe_route_kernel, shift_row=shift_row, scale_row=scale_row)
    return pl.pallas_call(
        kern, grid=(m // tm,),
        in_specs=[pl.BlockSpec((tm, d), lambda t: (t, 0)), pl.BlockSpec((1, d), lambda t: (0, 0)),
                  pl.BlockSpec((1, SUBLANES, d), lambda t: ((t * tm) // seg_tokens, 0, 0)),
                  pl.BlockSpec((d, LANES), lambda t: (0, 0)), pl.BlockSpec((d, LANES), lambda t: (0, 0)),
                  pl.BlockSpec((1, LANES), lambda t: (0, 0)), pl.BlockSpec((tm, tm), lambda t: (0, 0))],
        out_specs=[pl.BlockSpec((tm, d), lambda t: (t, 0)), row(), row(), row(),
                   pl.BlockSpec((SUBLANES, LANES), lambda t: (0, 0))],
        out_shape=[sd((m, d), F32), sd((m, LANES), jnp.int32), sd((m, LANES), F32), sd((m, LANES), F32),
                   sd((SUBLANES, LANES), F32)],
        scratch_shapes=[pltpu.VMEM((1, LANES), F32)],
        compiler_params=_cparams(("arbitrary",)), name="moe_route",
    )(x, g, mod, wh, wl, b, tril)


def _row_copy(src, dst, s_row, d_row, sem):
    return pltpu.make_async_copy(src.at[pl.ds(s_row, 1)], dst.at[pl.ds(d_row, 1)], sem)


def _moe_dispatch_kernel(slot_ref, hn_ref, zeros_ref, xs_ref, sem, *, tm):
    del zeros_ref

    def issue(t, c):
        for k in range(TOP_K):
            _row_copy(hn_ref, xs_ref, t, slot_ref[t * TOP_K + k], sem).start()
        return c
    lax.fori_loop(0, tm, issue, 0)

    def drain(t, c):
        for k in range(TOP_K):
            _row_copy(hn_ref, xs_ref, t, slot_ref[t * TOP_K + k], sem).wait()
        return c
    lax.fori_loop(0, tm, drain, 0)


def _moe_dispatch(hn, slot_flat, cap, tm):
    m, d = hn.shape
    return pl.pallas_call(
        functools.partial(_moe_dispatch_kernel, tm=tm), grid=(m // tm,),
        in_specs=[pl.BlockSpec((tm * TOP_K,), lambda t: (t,), memory_space=pltpu.SMEM),
                  pl.BlockSpec((tm, d), lambda t: (t, 0)), pl.BlockSpec(memory_space=pl.ANY)],
        out_specs=pl.BlockSpec(memory_space=pl.ANY),
        out_shape=jax.ShapeDtypeStruct((cap, d), F32),
        scratch_shapes=[pltpu.SemaphoreType.DMA(())],
        input_output_aliases={2: 0},
        compiler_params=_cparams(("arbitrary",)), name="moe_dispatch",
    )(slot_flat, hn, jnp.zeros((cap, d), F32))


def _swiglu(gate, up):
    gate = jnp.minimum(gate, SWIGLU_LIMIT)
    up = jnp.clip(up, -SWIGLU_LIMIT, SWIGLU_LIMIT)
    return gate * _sigmoid(gate * SWIGLU_ALPHA) * (up + 1.0)


def _moe_gu_kernel(meta_ref, xs_ref, wg_ref, wu_ref, bg_ref, bu_ref, h_ref, wg_s, wu_s):
    i = pl.program_id(1)

    @pl.when(meta_ref[1, i] == 1)
    def _():
        wg_s[...] = wg_ref[...].astype(BF16)
        wu_s[...] = wu_ref[...].astype(BF16)

    @pl.when(meta_ref[2, i] == 1)
    def _():
        xb = xs_ref[...].astype(BF16)
        gate = _dot(xb, wg_s[...]) + bg_ref[...]
        up = _dot(xb, wu_s[...]) + bu_ref[...]
        h_ref[...] = _swiglu(gate, up).astype(h_ref.dtype)

    @pl.when(meta_ref[2, i] == 0)
    def _():
        h_ref[...] = jnp.zeros_like(h_ref)


def _moe_gu(xs, meta, w_gu, b_gu, layer, tm, tn):
    cap, d = xs.shape
    f = w_gu.shape[3] // 2
    nj = f // tn
    wspec = lambda off: pl.BlockSpec((None, None, d, tn), lambda j, i, mt: (layer, mt[0, i], 0, off + j))
    bspec = lambda off: pl.BlockSpec((None, None, 1, tn), lambda j, i, mt: (layer, mt[0, i], 0, off + j))
    grid_spec = pltpu.PrefetchScalarGridSpec(
        num_scalar_prefetch=1, grid=(nj, cap // tm),
        in_specs=[pl.BlockSpec((tm, d), lambda j, i, mt: (i, 0)), wspec(0), wspec(nj), bspec(0), bspec(nj)],
        out_specs=pl.BlockSpec((tm, tn), lambda j, i, mt: (i, j)),
        scratch_shapes=[pltpu.VMEM((d, tn), BF16), pltpu.VMEM((d, tn), BF16)])
    b4 = b_gu.reshape(b_gu.shape[0], b_gu.shape[1], 1, b_gu.shape[2])
    return pl.pallas_call(
        _moe_gu_kernel, grid_spec=grid_spec, out_shape=jax.ShapeDtypeStruct((cap, f), BF16),
        compiler_params=_cparams(("arbitrary", "arbitrary")), name="moe_gate_up",
    )(meta, xs, w_gu, w_gu, b4, b4)


def _moe_down_kernel(meta_ref, h_ref, w_ref, b_ref, y_ref, w_s):
    i = pl.program_id(1)

    @pl.when(meta_ref[1, i] == 1)
    def _():
        w_s[...] = w_ref[...].astype(BF16)

    @pl.when(meta_ref[2, i] == 1)
    def _():
        y_ref[...] = _dot(h_ref[...], w_s[...]) + b_ref[...]

    @pl.when(meta_ref[2, i] == 0)
    def _():
        y_ref[...] = jnp.zeros_like(y_ref)


def _moe_down(h, meta, w_down, b_down, layer, tm, tn):
    cap, f = h.shape
    d = w_down.shape[3]
    grid_spec = pltpu.PrefetchScalarGridSpec(
        num_scalar_prefetch=1, grid=(d // tn, cap // tm),
        in_specs=[pl.BlockSpec((tm, f), lambda j, i, mt: (i, 0)),
                  pl.BlockSpec((None, None, f, tn), lambda j, i, mt: (layer, mt[0, i], 0, j)),
                  pl.BlockSpec((None, None, 1, tn), lambda j, i, mt: (layer, mt[0, i], 0, j))],
        out_specs=pl.BlockSpec((tm, tn), lambda j, i, mt: (i, j)),
        scratch_shapes=[pltpu.VMEM((f, tn), BF16)])
    b4 = b_down.reshape(b_down.shape[0], b_down.shape[1], 1, b_down.shape[2])
    return pl.pallas_call(
        _moe_down_kernel, grid_spec=grid_spec, out_shape=jax.ShapeDtypeStruct((cap, d), F32),
        compiler_params=_cparams(("arbitrary", "arbitrary")), name="moe_down",
    )(meta, h, w_down, b4)


def _moe_combine_kernel(slot_ref, yb_ref, x_ref, prob_ref, mod_ref, o_ref, buf, sem, *, tm, gate_row):
    def issue(t, c):
        for k in range(TOP_K):
            _row_copy(yb_ref, buf.at[k], slot_ref[t * TOP_K + k], t, sem).start()
        return c
    lax.fori_loop(0, tm, issue, 0)

    def drain(t, c):
        for k in range(TOP_K):
            _row_copy(yb_ref, buf.at[k], slot_ref[t * TOP_K + k], t, sem).wait()
        return c
    lax.fori_loop(0, tm, drain, 0)

    pr = prob_ref[...]
    acc = pr[:, 0:1] * buf[0]
    for k in range(1, TOP_K):
        acc += pr[:, k:k + 1] * buf[k]
    g = mod_ref[0][gate_row:gate_row + 1, :]
    o_ref[...] = x_ref[...] + g * acc


def _moe_combine(yb, slot_flat, prob, x, mod, seg_tokens, gate_row, tm):
    m, d = x.shape
    kern = functools.partial(_moe_combine_kernel, tm=tm, gate_row=gate_row)
    return pl.pallas_call(
        kern, grid=(m // tm,),
        in_specs=[pl.BlockSpec((tm * TOP_K,), lambda t: (t,), memory_space=pltpu.SMEM),
                  pl.BlockSpec(memory_space=pl.ANY),
                  pl.BlockSpec((tm, d), lambda t: (t, 0)),
                  pl.BlockSpec((tm, LANES), lambda t: (t, 0)),
                  pl.BlockSpec((1, SUBLANES, d), lambda t: ((t * tm) // seg_tokens, 0, 0))],
        out_specs=pl.BlockSpec((tm, d), lambda t: (t, 0)),
        out_shape=jax.ShapeDtypeStruct((m, d), F32),
        scratch_shapes=[pltpu.VMEM((TOP_K, tm, d), F32), pltpu.SemaphoreType.DMA(())],
        compiler_params=_cparams(("arbitrary",)), name="moe_combine",
    )(slot_flat, yb, x, prob, mod)


def _moe_layer(x, g, mod, router_w, router_b, w_gu, b_gu, w_down, b_down, layer, seg_tokens):
    m, d = x.shape
    hn, idx, rank, prob, cnt = _moe_route(x, g, mod, router_w, router_b, seg_tokens, 3, 4, MOE_TOK_TM)
    counts = cnt[0, :N_EXPERTS].astype(jnp.int32)
    padded = (counts + MOE_TM - 1) // MOE_TM * MOE_TM
    pad_end = jnp.cumsum(padded)
    starts = pad_end - padded
    e4 = idx[:, :TOP_K]
    slot = starts[e4] + rank[:, :TOP_K].astype(jnp.int32)
    slot_flat = slot.reshape(-1)
    n_blocks = (m * TOP_K) // MOE_TM + N_EXPERTS
    cap = n_blocks * MOE_TM
    blk0 = jnp.arange(n_blocks, dtype=jnp.int32) * MOE_TM
    block_e = jnp.minimum(jnp.searchsorted(pad_end, blk0, side='right'), N_EXPERTS - 1).astype(jnp.int32)
    is_new = jnp.concatenate([jnp.ones((1,), jnp.int32), (block_e[1:] != block_e[:-1]).astype(jnp.int32)])
    used = (blk0 < pad_end[-1]).astype(jnp.int32)
    meta = jnp.stack([block_e, is_new, used])
    xs = _moe_dispatch(hn, slot_flat, cap, MOE_TOK_TM)
    h = _moe_gu(xs, meta, w_gu, b_gu, layer, MOE_TM, MOE_GU_TN)
    yb = _moe_down(h, meta, w_down, b_down, layer, MOE_TM, MOE_DOWN_TN)
    return _moe_combine(yb, slot_flat, prob, x, mod, seg_tokens, 5, MOE_TOK_TM)


def _split_last(x, sizes):
    cuts = [int(s) for s in np.cumsum(sizes)[:-1]]
    return jnp.split(x, cuts, axis=-1)


def _relayout_w_in(w):
    z, xbc, dt, rw, q, k, v, gate = _split_last(
        w, (SSD_INNER, SSD_XBC, 2 * SSD_HEADS, RWKV_COLS, NA_INNER, NA_INNER, NA_INNER, N_BRANCHES * D_MODEL))
    padc = lambda a, n: jnp.pad(a, ((0, 0), (0, n - a.shape[1])))
    parts = [z, padc(rw, RWKV_PAD), xbc, padc(dt, DT_PAD), q, k, padc(v, COL_GATE - COL_V), gate]
    out = jnp.concatenate(parts, axis=1)
    return padc(out, IN_COLS_PAD).astype(BF16)


def kernel(x_prompt, x_sample, cache_na_k, cache_na_v, state_ssd, state_rwkv, c, c_ctx, norm1_g, norm2_g, w_mod, b_mod, w_in, ssd_conv_w, ssd_conv_b, ssd_dt_bias, ssd_a_log, ssd_d, ssd_norm_g, rwkv_mu, rwkv_w0, rwkv_w2, rwkv_a0, rwkv_a2, rwkv_g2, rwkv_kk, rwkv_ka, rwkv_rk, rwkv_ln_g, rwkv_ln_b, na_rpb, w_branch, w_out, router_w, router_b, moe_w_gu, moe_b_gu, moe_w_down, moe_b_down, final_g):
    nb, seq, d = x_prompt.shape
    db, dseq, _ = x_sample.shape
    n_ctx = nb * seq
    n_lat = db * dseq
    assert n_ctx == dseq, "modulation segments assume context tokens fill exactly one latent-length segment"
    seg_tokens = dseq
    n_seg = 1 + db
    m = n_ctx + n_lat
    rwkv_tm = 256
    tables = _scan_tables(nb, seq // RWKV_CHUNK, db, dseq // RWKV_CHUNK)
    ssd_tables = _scan_tables(nb, seq // SSD_CHUNK, db, dseq // SSD_CHUNK)

    x = jnp.concatenate([x_prompt.reshape(n_ctx, d), x_sample.reshape(n_lat, d)], axis=0)
    cond = jnp.zeros((SUBLANES, d), F32).at[0].set(c_ctx).at[1:1 + db].set(c)

    ctx_k, ctx_v, ctx_ssd, ctx_rwkv = [], [], [], []
    for l in range(DEPTH):
        mod = _modulation(cond, w_mod[l], b_mod[l])
        mod = mod.reshape(SUBLANES, 6, d)[:n_seg]
        mod = jnp.pad(mod, ((0, 0), (0, SUBLANES - 6), (0, 0)))
        p = _norm_matmul(x, norm1_g[l][None], mod, _relayout_w_in(w_in[l]), seg_tokens, 0, 1, 1024, IN_TN)

        lw = _rwkv_layer_weights(rwkv_mu[l], rwkv_w0[l], rwkv_w2[l], rwkv_a0[l], rwkv_a2[l], rwkv_g2[l],
                                 rwkv_kk[l], rwkv_ka[l], rwkv_rk[l], rwkv_tm)
        lw['ln_g'] = rwkv_ln_g[l][None]
        lw['ln_b'] = rwkv_ln_b[l][None]
        s0 = jnp.concatenate([jnp.zeros((nb,) + state_rwkv.shape[2:], F32), state_rwkv[:, l]], axis=0)
        y_rwkv, s_rwkv = rwkv_mix(p, lw, _pair_blockdiag(s0), tables, tm=rwkv_tm, n_ctx_tokens=n_ctx,
                                  ctx_len=seq, lat_len=dseq, pairs=8)
        ctx_rwkv.append(_pair_unblock(s_rwkv[:nb]))

        s0 = jnp.concatenate([jnp.zeros((nb,) + state_ssd.shape[2:], F32), state_ssd[:, l]], axis=0)
        y_ssd, s_ssd = ssd_mix(p, ssd_conv_w[l], ssd_conv_b[l], ssd_dt_bias[l], ssd_a_log[l], ssd_d[l],
                               ssd_norm_g[l], s0, ssd_tables, tm=256, n_ctx_tokens=n_ctx, ctx_len=seq, lat_len=dseq)
        ctx_ssd.append(s_ssd[:nb])

        hd = (NA_HEADS, HEAD_DIM)
        ctx_k.append(p[:n_ctx, COL_K:COL_K + NA_INNER].reshape((nb, seq) + hd))
        ctx_v.append(p[:n_ctx, COL_V:COL_V + NA_INNER].reshape((nb, seq) + hd))
        y_na_c = _context_attention(p, nb, seq)
        past = cache_na_k.shape[2]
        y_na_l = _neighbourhood_attention(p, cache_na_k[:, l].reshape(db, past, NA_INNER),
                                          cache_na_v[:, l].reshape(db, past, NA_INNER),
                                          _na_bias_table(na_rpb[l], dseq // GRID_W), n_ctx // dseq, db, dseq)
        y_na = jnp.concatenate([y_na_c, y_na_l], axis=0)

        merged = _branch_merge((y_ssd, y_rwkv, y_na), w_branch[l].astype(BF16), p, 512, 512)
        x = _proj_residual(merged, w_out[l].astype(BF16), x, mod, seg_tokens, 2, 1024, 512)

        x = _moe_layer(x, norm2_g[l][None], mod, router_w[l], router_b[l], moe_w_gu, moe_b_gu,
                       moe_w_down, moe_b_down, l, seg_tokens)

    y = _final_norm(x, final_g[None], 1024)
    y_prompt = y[:n_ctx].reshape(nb, seq, d)
    y_sample = y[n_ctx:].reshape(db, dseq, d)
    return (y_prompt, y_sample, jnp.stack(ctx_k, axis=1), jnp.stack(ctx_v, axis=1),
            jnp.stack(ctx_ssd, axis=1), jnp.stack(ctx_rwkv, axis=1))
```
